```python
import jax, jax.numpy as jnp
from jax import lax
import numpy as np

D_MODEL = 2048
BATCH = 4
SEQ = 4096
DEPTH = 1

N_META = 16
D_MIX = D_MODEL
D_SC = D_MIX // 2
D_CF = D_MIX - D_SC
SC_WIDTH = 3
CF_WIDTH = 31
D_IN = 3 * D_SC + 2 * D_CF
D_FF = 5632
FFN_RES_SCALE = 0.5
EPS = 1e-6

kernel_name = "hymba_parallel_shortconv_conformer_macaron"


def rmsnorm(x, g):
    xf = x.astype(jnp.float32)
    y = xf * lax.rsqrt(jnp.mean(xf * xf, axis=-1, keepdims=True) + EPS)
    return (y * g.astype(jnp.float32)).astype(x.dtype)


def layernorm(x, g, b):
    xf = x.astype(jnp.float32)
    mu = jnp.mean(xf, axis=-1, keepdims=True)
    var = jnp.mean(jnp.square(xf - mu), axis=-1, keepdims=True)
    y = (xf - mu) * lax.rsqrt(var + EPS)
    return (y * g.astype(jnp.float32) + b.astype(jnp.float32)).astype(x.dtype)


def causal_dwconv(x, w):
    k, c = w.shape
    return lax.conv_general_dilated(
        x, w.astype(x.dtype)[:, None, :], window_strides=(1,), padding=[(k - 1, 0)],
        dimension_numbers=("NWC", "WIO", "NWC"), feature_group_count=c)


def swiglu(h, w_gate, w_up, w_down):
    return (jax.nn.silu(h @ w_gate) * (h @ w_up)) @ w_down


def setup_inputs(seed: int = 0) -> dict:
    key = jax.random.key(seed)
    ks = jax.random.split(key, 24)
    f32 = jnp.float32
    nrm = lambda k, shape, scale: (jax.random.normal(k, shape, f32) * scale).astype(f32)
    gain = lambda k, shape: 1.0 + 0.05 * jax.random.normal(k, shape, f32)
    L = DEPTH
    return {
        "x": jax.random.normal(ks[0], (BATCH, SEQ, D_MODEL), f32),
        "meta_tokens": nrm(ks[1], (N_META, D_MODEL), 1.0),
        "ffn1_norm": gain(ks[2], (L, D_MODEL)),
        "ffn1_w_gate": nrm(ks[3], (L, D_MODEL, D_FF), D_MODEL ** -0.5),
        "ffn1_w_up": nrm(ks[4], (L, D_MODEL, D_FF), D_MODEL ** -0.5),
        "ffn1_w_down": nrm(ks[5], (L, D_FF, D_MODEL), D_FF ** -0.5),
        "mix_norm": gain(ks[6], (L, D_MODEL)),
        "w_in": nrm(ks[7], (L, D_MODEL, D_IN), D_MODEL ** -0.5),
        "b_in": nrm(ks[8], (L, D_IN), 0.02),
        "conv_sc_w": nrm(ks[9], (L, SC_WIDTH, D_SC), SC_WIDTH ** -0.5),
        "conv_cf_w": nrm(ks[10], (L, CF_WIDTH, D_CF), CF_WIDTH ** -0.5),
        "conv_cf_b": nrm(ks[11], (L, D_CF), 0.02),
        "ln_cf_g": gain(ks[12], (L, D_CF)),
        "ln_cf_b": nrm(ks[13], (L, D_CF), 0.02),
        "w_out": nrm(ks[14], (L, D_MIX, D_MODEL), D_MIX ** -0.5),
        "ffn2_norm": gain(ks[15], (L, D_MODEL)),
        "ffn2_w_gate": nrm(ks[16], (L, D_MODEL, D_FF), D_MODEL ** -0.5),
        "ffn2_w_up": nrm(ks[17], (L, D_MODEL, D_FF), D_MODEL ** -0.5),
        "ffn2_w_down": nrm(ks[18], (L, D_FF, D_MODEL), D_FF ** -0.5),
        "final_norm": gain(ks[19], (D_MODEL,)),
    }


def token_mixing(h, w_in, b_in, conv_sc_w, conv_cf_w, conv_cf_b, ln_cf_g, ln_cf_b, w_out):
    u = h @ w_in + b_in.astype(h.dtype)
    b_sc = u[..., :D_SC]
    c_sc = u[..., D_SC:2 * D_SC]
    v_sc = u[..., 2 * D_SC:3 * D_SC]
    a_cf = u[..., 3 * D_SC:3 * D_SC + D_CF]
    g_cf = u[..., 3 * D_SC + D_CF:]
    y_sc = b_sc * causal_dwconv(c_sc * v_sc, conv_sc_w)
    z = a_cf * jax.nn.sigmoid(g_cf)
    z = causal_dwconv(z, conv_cf_w) + conv_cf_b.astype(z.dtype)
    y_cf = jax.nn.silu(layernorm(z, ln_cf_g, ln_cf_b))
    return jnp.concatenate([y_sc, y_cf], axis=-1) @ w_out


def reference(x, meta_tokens, ffn1_norm, ffn1_w_gate, ffn1_w_up, ffn1_w_down, mix_norm, w_in, b_in,
              conv_sc_w, conv_cf_w, conv_cf_b, ln_cf_g, ln_cf_b, w_out,
              ffn2_norm, ffn2_w_gate, ffn2_w_up, ffn2_w_down, final_norm):
    bsz = x.shape[0]
    meta = jnp.broadcast_to(meta_tokens.astype(x.dtype)[None], (bsz, N_META, x.shape[-1]))
    hs = jnp.concatenate([meta, x], axis=1)
    for l in range(DEPTH):
        hs = hs + FFN_RES_SCALE * swiglu(rmsnorm(hs, ffn1_norm[l]), ffn1_w_gate[l], ffn1_w_up[l], ffn1_w_down[l])
        hs = hs + token_mixing(rmsnorm(hs, mix_norm[l]), w_in[l], b_in[l], conv_sc_w[l], conv_cf_w[l],
                               conv_cf_b[l], ln_cf_g[l], ln_cf_b[l], w_out[l])
        hs = hs + FFN_RES_SCALE * swiglu(rmsnorm(hs, ffn2_norm[l]), ffn2_w_gate[l], ffn2_w_up[l], ffn2_w_down[l])
    out = rmsnorm(hs, final_norm)
    return out[:, N_META:]
```

```python
import functools

import jax
import jax.numpy as jnp
from jax import lax
from jax.experimental import pallas as pl
from jax.experimental.pallas import tpu as pltpu

N_META = 16
SC_WIDTH = 3
CF_WIDTH = 31
FFN_RES_SCALE = 0.5
EPS = 1e-6

FFN_ROW_CHUNK = 256
HALO = 32
V7X_VMEM_BYTES = 64 * 1024 * 1024
VMEM_LIMIT = V7X_VMEM_BYTES - 8 * 1024 * 1024

F32 = jnp.float32
BF16 = jnp.bfloat16


def _rmsnorm_f32(x, g):
    ms = jnp.mean(x * x, axis=-1, keepdims=True)
    return x * lax.rsqrt(ms + EPS) * g


def _ffn_body(x_ref, g_ref, wg_ref, wu_ref, wd_ref, *rest, n_f, tm, rc, final):
    if final:
        fg_ref, o_ref, xn_ref = rest
    else:
        o_ref, xn_ref = rest
    f = pl.program_id(1)

    @pl.when(f == 0)
    def _():
        xn_ref[...] = _rmsnorm_f32(x_ref[...], g_ref[...]).astype(BF16)
        o_ref[...] = jnp.zeros_like(o_ref)

    for r0 in range(0, tm, rc):
        xn = xn_ref[r0:r0 + rc, :]
        gate = jnp.dot(xn, wg_ref[...], preferred_element_type=F32)
        up = jnp.dot(xn, wu_ref[...], preferred_element_type=F32)
        h = (gate * jax.nn.sigmoid(gate) * up).astype(BF16)
        o_ref[r0:r0 + rc, :] += jnp.dot(h, wd_ref[...], preferred_element_type=F32)

    @pl.when(f == n_f - 1)
    def _():
        y = x_ref[...] + FFN_RES_SCALE * o_ref[...]
        if final:
            y = _rmsnorm_f32(y, fg_ref[...])
        o_ref[...] = y


def _ffn(x, norm_g, w_gate, w_up, w_down, final_g=None, *, tm, tf, name):
    m, d = x.shape
    d_ff = w_gate.shape[1]
    n_f = d_ff // tf
    assert m % tm == 0 and d_ff % tf == 0
    final = final_g is not None
    in_specs = [
        pl.BlockSpec((tm, d), lambda i, f: (i, 0), pipeline_mode=pl.Buffered(1)),
        pl.BlockSpec((1, d), lambda i, f: (0, 0)),
        pl.BlockSpec((d, tf), lambda i, f: (0, f)),
        pl.BlockSpec((d, tf), lambda i, f: (0, f)),
        pl.BlockSpec((tf, d), lambda i, f: (f, 0)),
    ]
    args = [x, norm_g.reshape(1, d), w_gate, w_up, w_down]
    if final:
        in_specs.append(pl.BlockSpec((1, d), lambda i, f: (0, 0)))
        args.append(final_g.reshape(1, d))
    return pl.pallas_call(
        functools.partial(_ffn_body, n_f=n_f, tm=tm, rc=min(tm, FFN_ROW_CHUNK), final=final),
        grid=(m // tm, n_f),
        in_specs=in_specs,
        out_specs=pl.BlockSpec((tm, d), lambda i, f: (i, 0)),
        out_shape=jax.ShapeDtypeStruct((m, d), F32),
        scratch_shapes=[pltpu.VMEM((tm, d), BF16)],
        compiler_params=pltpu.CompilerParams(
            dimension_semantics=("arbitrary", "arbitrary"),
            vmem_limit_bytes=VMEM_LIMIT),
        name=name,
    )(*args)


def _inproj_body(x_ref, g_ref, wb_ref, wc_ref, wv_ref, wa_ref, wgt_ref, bias_ref,
                 b_ref, cv_ref, z_ref, xn_ref):
    j = pl.program_id(1)

    @pl.when(j == 0)
    def _():
        xn_ref[...] = _rmsnorm_f32(x_ref[...], g_ref[...]).astype(BF16)

    xn = xn_ref[...]

    def proj(w_ref, s):
        return jnp.dot(xn, w_ref[...], preferred_element_type=F32) + bias_ref[s:s + 1, :]

    b_ref[...] = proj(wb_ref, 0)
    cv_ref[...] = proj(wc_ref, 1) * proj(wv_ref, 2)
    z_ref[...] = proj(wa_ref, 3) * jax.nn.sigmoid(proj(wgt_ref, 4))


def _in_proj(x, norm_g, w_in, b_in, *, tm, tn, name):
    m, d = x.shape
    d_seg = w_in.shape[1] // 5
    n_j = d_seg // tn
    assert m % tm == 0 and d_seg % tn == 0
    w_specs = [pl.BlockSpec((d, tn), functools.partial(lambda i, j, s: (0, s * n_j + j), s=s))
               for s in range(5)]
    out_spec = pl.BlockSpec((tm, tn), lambda i, j: (i, j))
    out_sds = jax.ShapeDtypeStruct((m, d_seg), F32)
    return pl.pallas_call(
        _inproj_body,
        grid=(m // tm, n_j),
        in_specs=[pl.BlockSpec((tm, d), lambda i, j: (i, 0)),
                  pl.BlockSpec((1, d), lambda i, j: (0, 0)),
                  *w_specs,
                  pl.BlockSpec((5, tn), lambda i, j: (0, j))],
        out_specs=[out_spec, out_spec, out_spec],
        out_shape=[out_sds, out_sds, out_sds],
        scratch_shapes=[pltpu.VMEM((tm, d), BF16)],
        compiler_params=pltpu.CompilerParams(
            dimension_semantics=("arbitrary", "arbitrary"),
            vmem_limit_bytes=VMEM_LIMIT),
        name=name,
    )(x, norm_g.reshape(1, d), w_in, w_in, w_in, w_in, w_in, b_in.reshape(5, d_seg))


def _mixout_body(hs_ref, b_ref, cv_ref, cvp_ref, cvm_ref, z_ref, zp_ref, zm_ref,
                 wsc_ref, wcf_ref, bcf_ref, lng_ref, lnb_ref, wo_ref,
                 o_ref, cvbuf, zbuf, ybuf, *, tm, tiles_per_seq, rc):
    il = pl.program_id(0) % tiles_per_seq

    @pl.when(il == 0)
    def _():
        cvbuf[0:HALO, :] = cvm_ref[...]
        zbuf[0:HALO, :] = zm_ref[...]

    @pl.when(il != 0)
    def _():
        cvbuf[0:HALO, :] = cvp_ref[...]
        zbuf[0:HALO, :] = zp_ref[...]

    cvbuf[HALO:, :] = cv_ref[...]
    zbuf[HALO:, :] = z_ref[...]

    d_seg = z_ref.shape[1]
    inv_n = 1.0 / d_seg
    lc = 128
    for r0 in range(0, tm, rc):
        for c0 in range(0, d_seg, lc):
            acc = None
            for k in range(SC_WIDTH):
                off = HALO - (SC_WIDTH - 1) + k + r0
                t = cvbuf[off:off + rc, c0:c0 + lc] * wsc_ref[k:k + 1, c0:c0 + lc]
                acc = t if acc is None else acc + t
            ybuf[r0:r0 + rc, c0:c0 + lc] = (b_ref[r0:r0 + rc, c0:c0 + lc] * acc).astype(BF16)

        s1 = jnp.zeros((rc, lc), F32)
        chunks = []
        for c0 in range(0, d_seg, lc):
            acc = None
            for k in range(CF_WIDTH):
                off = HALO - (CF_WIDTH - 1) + k + r0
                t = zbuf[off:off + rc, c0:c0 + lc] * wcf_ref[k:k + 1, c0:c0 + lc]
                acc = t if acc is None else acc + t
            acc = acc + bcf_ref[:, c0:c0 + lc]
            chunks.append(acc)
            s1 = s1 + acc
        mu = jnp.sum(s1, axis=-1, keepdims=True) * inv_n
        s2 = jnp.zeros((rc, lc), F32)
        for i in range(len(chunks)):
            chunks[i] = chunks[i] - mu
            s2 = s2 + chunks[i] * chunks[i]
        var = jnp.sum(s2, axis=-1, keepdims=True) * inv_n
        rstd = lax.rsqrt(var + EPS)
        for i, c0 in enumerate(range(0, d_seg, lc)):
            yn = chunks[i] * rstd * lng_ref[:, c0:c0 + lc] + lnb_ref[:, c0:c0 + lc]
            ybuf[r0:r0 + rc, d_seg + c0:d_seg + c0 + lc] = (yn * jax.nn.sigmoid(yn)).astype(BF16)

    o_ref[...] = hs_ref[...] + jnp.dot(ybuf[...], wo_ref[...], preferred_element_type=F32)


def _mix_out(hs, b, cv, z, cv_meta, z_meta, conv_sc_w, conv_cf_w, conv_cf_b, ln_g, ln_b, w_out,
             *, seq, tm, rc, name):
    m, d = hs.shape
    d_seg = z.shape[1]
    assert seq % tm == 0 and tm % HALO == 0 and tm % rc == 0
    tiles_per_seq = seq // tm
    hb = tm // HALO
    tile = pl.BlockSpec((tm, d_seg), lambda i: (i, 0))
    prev = pl.BlockSpec((HALO, d_seg), lambda i: (jnp.maximum(i * hb - 1, 0), 0))
    full = lambda shape: pl.BlockSpec(shape, lambda i: (0, 0))
    return pl.pallas_call(
        functools.partial(_mixout_body, tm=tm, tiles_per_seq=tiles_per_seq, rc=rc),
        grid=(m // tm,),
        in_specs=[pl.BlockSpec((tm, d), lambda i: (i, 0)),
                  tile, tile, prev, full((HALO, d_seg)), tile, prev, full((HALO, d_seg)),
                  full((SC_WIDTH, d_seg)), full((CF_WIDTH, d_seg)),
                  full((1, d_seg)), full((1, d_seg)), full((1, d_seg)),
                  pl.BlockSpec((2 * d_seg, d), lambda i: (0, 0), pipeline_mode=pl.Buffered(1))],
        out_specs=pl.BlockSpec((tm, d), lambda i: (i, 0)),
        out_shape=jax.ShapeDtypeStruct((m, d), F32),
        scratch_shapes=[pltpu.VMEM((HALO + tm, d_seg), F32),
                        pltpu.VMEM((HALO + tm, d_seg), F32),
                        pltpu.VMEM((tm, 2 * d_seg), BF16)],
        compiler_params=pltpu.CompilerParams(
            dimension_semantics=("arbitrary",),
            vmem_limit_bytes=VMEM_LIMIT),
        name=name,
    )(hs, b, cv, cv, cv_meta, z, z, z_meta, conv_sc_w, conv_cf_w,
      conv_cf_b.reshape(1, d_seg), ln_g.reshape(1, d_seg), ln_b.reshape(1, d_seg), w_out)


def kernel(x, meta_tokens, ffn1_norm, ffn1_w_gate, ffn1_w_up, ffn1_w_down, mix_norm, w_in, b_in,
           conv_sc_w, conv_cf_w, conv_cf_b, ln_cf_g, ln_cf_b, w_out,
           ffn2_norm, ffn2_w_gate, ffn2_w_up, ffn2_w_down, final_norm):
    bsz, seq, d = x.shape
    d_seg = conv_sc_w.shape[-1]
    assert ffn1_norm.shape[0] == 1 and meta_tokens.shape[0] == N_META
    hs = x.reshape(bsz * seq, d)
    hm = meta_tokens.astype(x.dtype)
    w1 = [w[0].astype(BF16) for w in (ffn1_w_gate, ffn1_w_up, ffn1_w_down)]
    w2 = [w[0].astype(BF16) for w in (ffn2_w_gate, ffn2_w_up, ffn2_w_down)]
    wi = w_in[0].astype(BF16)
    wo = w_out[0].astype(BF16)

    hs = _ffn(hs, ffn1_norm[0], *w1, tm=1024, tf=512, name="ffn1")
    hm = _ffn(hm, ffn1_norm[0], *w1, tm=N_META, tf=512, name="ffn1_meta")

    b, cv, z = _in_proj(hs, mix_norm[0], wi, b_in[0], tm=1024, tn=256, name="in_proj")
    _, cvm, zm = _in_proj(hm, mix_norm[0], wi, b_in[0], tm=N_META, tn=256, name="in_proj_meta")
    pad = jnp.zeros((HALO - N_META, d_seg), F32)
    cv_halo = jnp.concatenate([pad, cvm], axis=0)
    z_halo = jnp.concatenate([pad, zm], axis=0)

    hs = _mix_out(hs, b, cv, z, cv_halo, z_halo, conv_sc_w[0], conv_cf_w[0], conv_cf_b[0],
                  ln_cf_g[0], ln_cf_b[0], wo, seq=seq, tm=512, rc=64, name="mix_out")
    out = _ffn(hs, ffn2_norm[0], *w2, final_norm, tm=1024, tf=512, name="ffn2_final")
    return out.reshape(bsz, seq, d)
```

```python
import functools

import jax
import jax.numpy as jnp
from jax import lax
from jax.experimental import pallas as pl
from jax.experimental.pallas import tpu as pltpu

N_META = 16
SC_WIDTH = 3
CF_WIDTH = 31
FFN_RES_SCALE = 0.5
EPS = 1e-6

SUBLANES = 8
LANES = 128
FFN_ROW_CHUNK = 256
CONV_ROW_CHUNK = 128
HALO = 32
V7X_VMEM_BYTES = 64 * 1024 * 1024
VMEM_LIMIT = V7X_VMEM_BYTES - 8 * 1024 * 1024

F32 = jnp.float32
BF16 = jnp.bfloat16


def _rmsnorm_f32(x, g):
    ms = jnp.mean(x * x, axis=-1, keepdims=True)
    return x * lax.rsqrt(ms + EPS) * g


def _ffn_body(x_hbm, g_ref, wg_hbm, wu_hbm, wd_hbm, *rest, n_tiles, n_f, tm, tf, rc, final):
    if final:
        fg_ref, o_hbm, acc, xn_ref, wg_buf, wu_buf, wd_buf, xsem, osem, wsem = rest
    else:
        o_hbm, acc, xn_ref, wg_buf, wu_buf, wd_buf, xsem, osem, wsem = rest
    i = pl.program_id(0)
    slot = i % 2

    def x_copy(tile, s):
        return pltpu.make_async_copy(x_hbm.at[pl.ds(tile * tm, tm), :], acc.at[s], xsem.at[s])

    def o_copy(tile, s):
        return pltpu.make_async_copy(acc.at[s], o_hbm.at[pl.ds(tile * tm, tm), :], osem.at[s])

    def w_copies(f, s):
        return (pltpu.make_async_copy(wg_hbm.at[:, pl.ds(f * tf, tf)], wg_buf.at[s], wsem.at[0, s]),
                pltpu.make_async_copy(wu_hbm.at[:, pl.ds(f * tf, tf)], wu_buf.at[s], wsem.at[1, s]),
                pltpu.make_async_copy(wd_hbm.at[pl.ds(f * tf, tf), :], wd_buf.at[s], wsem.at[2, s]))

    @pl.when(i == 0)
    def _():
        x_copy(0, 0).start()
        for c in w_copies(0, 0):
            c.start()

    x_copy(i, slot).wait()
    xn_ref[...] = _rmsnorm_f32(acc[slot], g_ref[...]).astype(BF16)

    def f_step(f, carry):
        wslot = (i * n_f + f) % 2
        for c in w_copies(f, wslot):
            c.wait()

        @pl.when(jnp.logical_or(f + 1 < n_f, i + 1 < n_tiles))
        def _():
            for c in w_copies(jnp.where(f + 1 < n_f, f + 1, 0), 1 - wslot):
                c.start()

        @pl.when(jnp.logical_and(f == min(1, n_f - 1), i + 1 < n_tiles))
        def _():
            @pl.when(i >= 1)
            def _():
                o_copy(i - 1, 1 - slot).wait()
            x_copy(i + 1, 1 - slot).start()

        for r0 in range(0, tm, rc):
            xn = xn_ref[r0:r0 + rc, :]
            gate = jnp.dot(xn, wg_buf[wslot], preferred_element_type=F32)
            up = jnp.dot(xn, wu_buf[wslot], preferred_element_type=F32)
            h = (gate * jax.nn.sigmoid(gate) * up).astype(BF16)
            acc[slot, r0:r0 + rc, :] += FFN_RES_SCALE * jnp.dot(h, wd_buf[wslot], preferred_element_type=F32)
        return carry

    lax.fori_loop(0, n_f, f_step, 0)

    if final:
        acc[slot] = _rmsnorm_f32(acc[slot], fg_ref[...])
    o_copy(i, slot).start()

    @pl.when(i == n_tiles - 1)
    def _():
        if n_tiles > 1:
            o_copy(i - 1, 1 - slot).wait()
        o_copy(i, slot).wait()


def _ffn(x, norm_g, w_gate, w_up, w_down, final_g=None, *, tm, tf, name):
    m, d = x.shape
    d_ff = w_gate.shape[1]
    assert m % tm == 0 and d_ff % tf == 0
    n_tiles, n_f = m // tm, d_ff // tf
    final = final_g is not None
    any_spec = pl.BlockSpec(memory_space=pl.ANY)
    row_spec = pl.BlockSpec((1, d), lambda i: (0, 0))
    in_specs = [any_spec, row_spec, any_spec, any_spec, any_spec]
    args = [x, norm_g.reshape(1, d), w_gate, w_up, w_down]
    if final:
        in_specs.append(row_spec)
        args.append(final_g.reshape(1, d))
    return pl.pallas_call(
        functools.partial(_ffn_body, n_tiles=n_tiles, n_f=n_f, tm=tm, tf=tf,
                          rc=min(tm, FFN_ROW_CHUNK), final=final),
        grid=(n_tiles,),
        in_specs=in_specs,
        out_specs=any_spec,
        out_shape=jax.ShapeDtypeStruct((m, d), F32),
        scratch_shapes=[pltpu.VMEM((2, tm, d), F32),
                        pltpu.VMEM((tm, d), BF16),
                        pltpu.VMEM((2, d, tf), BF16),
                        pltpu.VMEM((2, d, tf), BF16),
                        pltpu.VMEM((2, tf, d), BF16),
                        pltpu.SemaphoreType.DMA((2,)),
                        pltpu.SemaphoreType.DMA((2,)),
                        pltpu.SemaphoreType.DMA((3, 2))],
        compiler_params=pltpu.CompilerParams(
            dimension_semantics=("arbitrary",),
            vmem_limit_bytes=VMEM_LIMIT),
        name=name,
    )(*args)


def _inproj_body(x_ref, g_ref, wb_ref, wc_ref, wv_ref, wa_ref, wgt_ref, bias_ref,
                 b_ref, cv_ref, z_ref, xn_ref):
    j = pl.program_id(1)

    @pl.when(j == 0)
    def _():
        xn_ref[...] = _rmsnorm_f32(x_ref[...], g_ref[...]).astype(BF16)

    xn = xn_ref[...]

    def proj(w_ref, s):
        return jnp.dot(xn, w_ref[...], preferred_element_type=F32) + bias_ref[s:s + 1, :]

    b_ref[...] = proj(wb_ref, 0)
    cv_ref[...] = proj(wc_ref, 1) * proj(wv_ref, 2)
    z_ref[...] = proj(wa_ref, 3) * jax.nn.sigmoid(proj(wgt_ref, 4))


def _in_proj(x, norm_g, w_in, b_in, *, tm, tn, name):
    m, d = x.shape
    d_seg = w_in.shape[1] // 5
    n_j = d_seg // tn
    assert m % tm == 0 and d_seg % tn == 0
    w_specs = [pl.BlockSpec((d, tn), functools.partial(lambda i, j, s: (0, s * n_j + j), s=s))
               for s in range(5)]
    out_spec = pl.BlockSpec((tm, tn), lambda i, j: (i, j))
    out_sds = jax.ShapeDtypeStruct((m, d_seg), F32)
    return pl.pallas_call(
        _inproj_body,
        grid=(m // tm, n_j),
        in_specs=[pl.BlockSpec((tm, d), lambda i, j: (i, 0)),
                  pl.BlockSpec((1, d), lambda i, j: (0, 0)),
                  *w_specs,
                  pl.BlockSpec((5, tn), lambda i, j: (0, j))],
        out_specs=[out_spec, out_spec, out_spec],
        out_shape=[out_sds, out_sds, out_sds],
        scratch_shapes=[pltpu.VMEM((tm, d), BF16)],
        compiler_params=pltpu.CompilerParams(
            dimension_semantics=("arbitrary", "arbitrary"),
            vmem_limit_bytes=VMEM_LIMIT),
        name=name,
    )(x, norm_g.reshape(1, d), w_in, w_in, w_in, w_in, w_in, b_in.reshape(5, d_seg))


def _causal_taps(buf, w_ref, r0, c0, rc, width):
    base = HALO - (width - 1)
    out = None
    for s in range(SUBLANES):
        ks = [k for k in range(width) if (base + k) % SUBLANES == s]
        if not ks:
            continue
        rows = rc if s == 0 else rc + SUBLANES
        u = None
        for k in ks:
            a = base + k - s + r0
            t = buf[a:a + rows, c0:c0 + LANES] * w_ref[k:k + 1, c0:c0 + LANES]
            u = t if u is None else u + t
        if s:
            u = u[s:s + rc]
        out = u if out is None else out + u
    return out


def _mixout_body(hs_ref, b_ref, cv_ref, cvp_ref, cvm_ref, z_ref, zp_ref, zm_ref,
                 wsc_ref, wcf_ref, bcf_ref, lng_ref, lnb_ref, wo_ref,
                 o_ref, cvbuf, zbuf, ybuf, cbuf, *, tm, tiles_per_seq, rc):
    il = pl.program_id(0) % tiles_per_seq

    @pl.when(il == 0)
    def _():
        cvbuf[0:HALO, :] = cvm_ref[...]
        zbuf[0:HALO, :] = zm_ref[...]

    @pl.when(il != 0)
    def _():
        cvbuf[0:HALO, :] = cvp_ref[...]
        zbuf[0:HALO, :] = zp_ref[...]

    cvbuf[HALO:, :] = cv_ref[...]
    zbuf[HALO:, :] = z_ref[...]

    d_seg = z_ref.shape[1]
    inv_n = 1.0 / d_seg
    lane_chunks = range(0, d_seg, LANES)
    for r0 in range(0, tm, rc):
        rows = slice(r0, r0 + rc)
        for c0 in lane_chunks:
            acc = _causal_taps(cvbuf, wsc_ref, r0, c0, rc, SC_WIDTH)
            ybuf[rows, c0:c0 + LANES] = (b_ref[rows, c0:c0 + LANES] * acc).astype(BF16)

        s1 = jnp.zeros((rc, LANES), F32)
        for c0 in lane_chunks:
            acc = _causal_taps(zbuf, wcf_ref, r0, c0, rc, CF_WIDTH) + bcf_ref[:, c0:c0 + LANES]
            cbuf[rows, c0:c0 + LANES] = acc
            s1 = s1 + acc
        mu = jnp.sum(s1, axis=-1, keepdims=True) * inv_n
        s2 = jnp.zeros((rc, LANES), F32)
        for c0 in lane_chunks:
            dlt = cbuf[rows, c0:c0 + LANES] - mu
            s2 = s2 + dlt * dlt
        var = jnp.sum(s2, axis=-1, keepdims=True) * inv_n
        rstd = lax.rsqrt(var + EPS)
        for c0 in lane_chunks:
            yn = (cbuf[rows, c0:c0 + LANES] - mu) * rstd * lng_ref[:, c0:c0 + LANES] + lnb_ref[:, c0:c0 + LANES]
            ybuf[rows, d_seg + c0:d_seg + c0 + LANES] = (yn * jax.nn.sigmoid(yn)).astype(BF16)

        o_ref[rows, :] = hs_ref[rows, :] + jnp.dot(ybuf[rows, :], wo_ref[...], preferred_element_type=F32)


def _mix_out(hs, b, cv, z, cv_meta, z_meta, conv_sc_w, conv_cf_w, conv_cf_b, ln_g, ln_b, w_out,
             *, seq, tm, name):
    m, d = hs.shape
    d_seg = z.shape[1]
    rc = CONV_ROW_CHUNK
    assert seq % tm == 0 and tm % HALO == 0 and tm % rc == 0
    tiles_per_seq = seq // tm
    hb = tm // HALO
    tile = pl.BlockSpec((tm, d_seg), lambda i: (i, 0))
    prev = pl.BlockSpec((HALO, d_seg), lambda i: (jnp.maximum(i * hb - 1, 0), 0))
    full = lambda shape: pl.BlockSpec(shape, lambda i: (0, 0))
    return pl.pallas_call(
        functools.partial(_mixout_body, tm=tm, tiles_per_seq=tiles_per_seq, rc=rc),
        grid=(m // tm,),
        in_specs=[pl.BlockSpec((tm, d), lambda i: (i, 0)),
                  tile, tile, prev, full((HALO, d_seg)), tile, prev, full((HALO, d_seg)),
                  full((SC_WIDTH, d_seg)), full((CF_WIDTH, d_seg)),
                  full((1, d_seg)), full((1, d_seg)), full((1, d_seg)),
                  pl.BlockSpec((2 * d_seg, d), lambda i: (0, 0), pipeline_mode=pl.Buffered(1))],
        out_specs=pl.BlockSpec((tm, d), lambda i: (i, 0)),
        out_shape=jax.ShapeDtypeStruct((m, d), F32),
        scratch_shapes=[pltpu.VMEM((HALO + tm, d_seg), F32),
                        pltpu.VMEM((HALO + tm, d_seg), F32),
                        pltpu.VMEM((tm, 2 * d_seg), BF16),
                        pltpu.VMEM((tm, d_seg), F32)],
        compiler_params=pltpu.CompilerParams(
            dimension_semantics=("arbitrary",),
            vmem_limit_bytes=VMEM_LIMIT),
        name=name,
    )(hs, b, cv, cv, cv_meta, z, z, z_meta, conv_sc_w, conv_cf_w,
      conv_cf_b.reshape(1, d_seg), ln_g.reshape(1, d_seg), ln_b.reshape(1, d_seg), w_out)


def kernel(x, meta_tokens, ffn1_norm, ffn1_w_gate, ffn1_w_up, ffn1_w_down, mix_norm, w_in, b_in,
           conv_sc_w, conv_cf_w, conv_cf_b, ln_cf_g, ln_cf_b, w_out,
           ffn2_norm, ffn2_w_gate, ffn2_w_up, ffn2_w_down, final_norm):
    bsz, seq, d = x.shape
    d_seg = conv_sc_w.shape[-1]
    assert ffn1_norm.shape[0] == 1 and meta_tokens.shape[0] == N_META
    hs = x.reshape(bsz * seq, d)
    hm = meta_tokens.astype(x.dtype)
    w1 = [w[0].astype(BF16) for w in (ffn1_w_gate, ffn1_w_up, ffn1_w_down)]
    w2 = [w[0].astype(BF16) for w in (ffn2_w_gate, ffn2_w_up, ffn2_w_down)]
    wi = w_in[0].astype(BF16)
    wo = w_out[0].astype(BF16)

    hs = _ffn(hs, ffn1_norm[0], *w1, tm=1024, tf=512, name="ffn1")
    hm = _ffn(hm, ffn1_norm[0], *w1, tm=N_META, tf=512, name="ffn1_meta")

    b, cv, z = _in_proj(hs, mix_norm[0], wi, b_in[0], tm=1024, tn=256, name="in_proj")
    _, cvm, zm = _in_proj(hm, mix_norm[0], wi, b_in[0], tm=N_META, tn=256, name="in_proj_meta")
    pad = jnp.zeros((HALO - N_META, d_seg), F32)
    cv_halo = jnp.concatenate([pad, cvm], axis=0)
    z_halo = jnp.concatenate([pad, zm], axis=0)

    hs = _mix_out(hs, b, cv, z, cv_halo, z_halo, conv_sc_w[0], conv_cf_w[0], conv_cf_b[0],
                  ln_cf_g[0], ln_cf_b[0], wo, seq=seq, tm=512, name="mix_out")
    out = _ffn(hs, ffn2_norm[0], *w2, final_norm, tm=1024, tf=512, name="ffn2_final")
    return out.reshape(bsz, seq, d)
```

```python
import functools

import jax
import jax.numpy as jnp
from jax import lax
from jax.experimental import pallas as pl
from jax.experimental.pallas import tpu as pltpu

N_META = 16
SC_WIDTH = 3
CF_WIDTH = 31
FFN_RES_SCALE = 0.5
EPS = 1e-6

SUBLANES = 8
LANES = 128
FFN_ROW_CHUNK = 256
FFN_TF = 512
CONV_ROW_CHUNK = 64
HALO = 32
V7X_VMEM_BYTES = 64 * 1024 * 1024
VMEM_LIMIT = V7X_VMEM_BYTES - 8 * 1024 * 1024

F32 = jnp.float32
BF16 = jnp.bfloat16


def _rmsnorm_f32(x, g):
    ms = jnp.mean(x * x, axis=-1, keepdims=True)
    return x * lax.rsqrt(ms + EPS) * g


def _ffn_body(x_hbm, g_ref, wg_hbm, wu_hbm, wd_hbm, *rest, n_tiles, n_f, tm, tf, rc, final):
    if final:
        fg_ref, o_hbm, acc, xn_ref, wg_buf, wu_buf, wd_buf, xsem, osem, wsem = rest
    else:
        o_hbm, acc, xn_ref, wg_buf, wu_buf, wd_buf, xsem, osem, wsem = rest
    i = pl.program_id(0)
    slot = i % 2

    def x_copy(tile, s):
        return pltpu.make_async_copy(x_hbm.at[pl.ds(tile * tm, tm), :], acc.at[s], xsem.at[s])

    def o_copy(tile, s):
        return pltpu.make_async_copy(acc.at[s], o_hbm.at[pl.ds(tile * tm, tm), :], osem.at[s])

    def w_copies(f, s):
        return (pltpu.make_async_copy(wg_hbm.at[:, pl.ds(f * tf, tf)], wg_buf.at[s], wsem.at[0, s]),
                pltpu.make_async_copy(wu_hbm.at[:, pl.ds(f * tf, tf)], wu_buf.at[s], wsem.at[1, s]),
                pltpu.make_async_copy(wd_hbm.at[pl.ds(f * tf, tf), :], wd_buf.at[s], wsem.at[2, s]))

    @pl.when(i == 0)
    def _():
        x_copy(0, 0).start()
        for c in w_copies(0, 0):
            c.start()

    x_copy(i, slot).wait()
    xn_ref[...] = _rmsnorm_f32(acc[slot], g_ref[...]).astype(BF16)

    def f_step(f, carry):
        wslot = (i * n_f + f) % 2
        for c in w_copies(f, wslot):
            c.wait()

        @pl.when(jnp.logical_or(f + 1 < n_f, i + 1 < n_tiles))
        def _():
            for c in w_copies(jnp.where(f + 1 < n_f, f + 1, 0), 1 - wslot):
                c.start()

        @pl.when(jnp.logical_and(f == min(1, n_f - 1), i + 1 < n_tiles))
        def _():
            @pl.when(i >= 1)
            def _():
                o_copy(i - 1, 1 - slot).wait()
            x_copy(i + 1, 1 - slot).start()

        for r0 in range(0, tm, rc):
            xn = xn_ref[r0:r0 + rc, :]
            gate = jnp.dot(xn, wg_buf[wslot].astype(BF16), preferred_element_type=F32)
            up = jnp.dot(xn, wu_buf[wslot].astype(BF16), preferred_element_type=F32)
            h = (gate * jax.nn.sigmoid(gate) * up).astype(BF16)
            acc[slot, r0:r0 + rc, :] += FFN_RES_SCALE * jnp.dot(h, wd_buf[wslot].astype(BF16),
                                                                 preferred_element_type=F32)
        return carry

    lax.fori_loop(0, n_f, f_step, 0)

    if final:
        acc[slot] = _rmsnorm_f32(acc[slot], fg_ref[...])
    o_copy(i, slot).start()

    @pl.when(i == n_tiles - 1)
    def _():
        if n_tiles > 1:
            o_copy(i - 1, 1 - slot).wait()
        o_copy(i, slot).wait()


def _ffn(x, norm_g, w_gate, w_up, w_down, final_g=None, *, tm, tf, name):
    m, d = x.shape
    d_ff = w_gate.shape[1]
    assert m % tm == 0 and d_ff % tf == 0
    n_tiles, n_f = m // tm, d_ff // tf
    final = final_g is not None
    any_spec = pl.BlockSpec(memory_space=pl.ANY)
    row_spec = pl.BlockSpec((1, d), lambda i: (0, 0))
    in_specs = [any_spec, row_spec, any_spec, any_spec, any_spec]
    args = [x, norm_g.reshape(1, d), w_gate, w_up, w_down]
    if final:
        in_specs.append(row_spec)
        args.append(final_g.reshape(1, d))
    return pl.pallas_call(
        functools.partial(_ffn_body, n_tiles=n_tiles, n_f=n_f, tm=tm, tf=tf,
                          rc=min(tm, FFN_ROW_CHUNK), final=final),
        grid=(n_tiles,),
        in_specs=in_specs,
        out_specs=any_spec,
        out_shape=jax.ShapeDtypeStruct((m, d), F32),
        scratch_shapes=[pltpu.VMEM((2, tm, d), F32),
                        pltpu.VMEM((tm, d), BF16),
                        pltpu.VMEM((2, d, tf), w_gate.dtype),
                        pltpu.VMEM((2, d, tf), w_up.dtype),
                        pltpu.VMEM((2, tf, d), w_down.dtype),
                        pltpu.SemaphoreType.DMA((2,)),
                        pltpu.SemaphoreType.DMA((2,)),
                        pltpu.SemaphoreType.DMA((3, 2))],
        compiler_params=pltpu.CompilerParams(
            dimension_semantics=("arbitrary",),
            vmem_limit_bytes=VMEM_LIMIT),
        name=name,
    )(*args)


def _inproj_body(x_ref, g_ref, wb_ref, wc_ref, wv_ref, wa_ref, wgt_ref, bias_ref,
                 b_ref, cv_ref, z_ref, xn_ref):
    j = pl.program_id(1)

    @pl.when(j == 0)
    def _():
        xn_ref[...] = _rmsnorm_f32(x_ref[...], g_ref[...]).astype(BF16)

    xn = xn_ref[...]

    def proj(w_ref, s):
        return jnp.dot(xn, w_ref[...], preferred_element_type=F32) + bias_ref[s:s + 1, :]

    b_ref[...] = proj(wb_ref, 0)
    cv_ref[...] = proj(wc_ref, 1) * proj(wv_ref, 2)
    z_ref[...] = proj(wa_ref, 3) * jax.nn.sigmoid(proj(wgt_ref, 4))


def _in_proj(x, norm_g, w_in, b_in, *, tm, tn, name):
    m, d = x.shape
    d_seg = w_in.shape[1] // 5
    n_j = d_seg // tn
    assert m % tm == 0 and d_seg % tn == 0
    w_specs = [pl.BlockSpec((d, tn), functools.partial(lambda i, j, s: (0, s * n_j + j), s=s))
               for s in range(5)]
    out_spec = pl.BlockSpec((tm, tn), lambda i, j: (i, j))
    out_sds = jax.ShapeDtypeStruct((m, d_seg), F32)
    return pl.pallas_call(
        _inproj_body,
        grid=(m // tm, n_j),
        in_specs=[pl.BlockSpec((tm, d), lambda i, j: (i, 0)),
                  pl.BlockSpec((1, d), lambda i, j: (0, 0)),
                  *w_specs,
                  pl.BlockSpec((5, tn), lambda i, j: (0, j))],
        out_specs=[out_spec, out_spec, out_spec],
        out_shape=[out_sds, out_sds, out_sds],
        scratch_shapes=[pltpu.VMEM((tm, d), BF16)],
        compiler_params=pltpu.CompilerParams(
            dimension_semantics=("arbitrary", "arbitrary"),
            vmem_limit_bytes=VMEM_LIMIT),
        name=name,
    )(x, norm_g.reshape(1, d), w_in, w_in, w_in, w_in, w_in, b_in.reshape(5, d_seg))


def _causal_taps(buf, w_ref, r0, c0, rc, width):
    base = HALO - (width - 1)
    out = None
    for s in range(SUBLANES):
        ks = [k for k in range(width) if (base + k) % SUBLANES == s]
        if not ks:
            continue
        rows = rc if s == 0 else rc + SUBLANES
        u = None
        for k in ks:
            a = base + k - s + r0
            t = buf[a:a + rows, c0:c0 + LANES] * w_ref[k:k + 1, c0:c0 + LANES]
            u = t if u is None else u + t
        if s:
            u = u[s:s + rc]
        out = u if out is None else out + u
    return out


def _mixer_body(x_ref, g_ref, win_ref, bias_ref, cvm_ref, zm_ref, wsc_ref, wcf_ref, bcf_ref, lng_ref, lnb_ref,
                y_ref, xn_ref, bring, cvring, zring, cbuf, *, n_tiles, tm, tn, tiles_per_seq, rc):
    i = pl.program_id(0)
    sp = i % 2
    sc = 1 - sp
    d_seg = zring.shape[2]

    @pl.when(i == 0)
    def _():
        bring[1] = jnp.zeros(bring.shape[1:], F32)
        cvring[1] = jnp.zeros(cvring.shape[1:], F32)
        zring[1] = jnp.zeros(zring.shape[1:], F32)

    first = jnp.minimum(i, n_tiles - 1) % tiles_per_seq == 0
    cvring[sp, 0:HALO, :] = jnp.where(first, cvm_ref[...], cvring[sc, tm:tm + HALO, :])
    zring[sp, 0:HALO, :] = jnp.where(first, zm_ref[...], zring[sc, tm:tm + HALO, :])

    xn_ref[...] = _rmsnorm_f32(x_ref[...], g_ref[...]).astype(BF16)

    def project(j):
        def proj(s):
            cols = slice(s * d_seg + j * tn, s * d_seg + (j + 1) * tn)
            return jnp.dot(xn_ref[...], win_ref[:, cols], preferred_element_type=F32) + bias_ref[:, cols]
        cols = slice(j * tn, (j + 1) * tn)
        bring[sp, :, cols] = proj(0)
        cvring[sp, HALO:, cols] = proj(1) * proj(2)
        zring[sp, HALO:, cols] = proj(3) * jax.nn.sigmoid(proj(4))

    cvbuf, zbuf, bbuf = cvring.at[sc], zring.at[sc], bring.at[sc]
    inv_n = 1.0 / d_seg
    lane_chunks = range(0, d_seg, LANES)

    def convolve(r0):
        rows = slice(r0, r0 + rc)
        for c0 in lane_chunks:
            acc = _causal_taps(cvbuf, wsc_ref, r0, c0, rc, SC_WIDTH)
            y_ref[rows, c0:c0 + LANES] = (bbuf[rows, c0:c0 + LANES] * acc).astype(BF16)

        s1 = jnp.zeros((rc, LANES), F32)
        for c0 in lane_chunks:
            acc = _causal_taps(zbuf, wcf_ref, r0, c0, rc, CF_WIDTH) + bcf_ref[:, c0:c0 + LANES]
            cbuf[rows, c0:c0 + LANES] = acc
            s1 = s1 + acc
        mu = jnp.sum(s1, axis=-1, keepdims=True) * inv_n
        s2 = jnp.zeros((rc, LANES), F32)
        for c0 in lane_chunks:
            dlt = cbuf[rows, c0:c0 + LANES] - mu
            s2 = s2 + dlt * dlt
        var = jnp.sum(s2, axis=-1, keepdims=True) * inv_n
        rstd = lax.rsqrt(var + EPS)
        for c0 in lane_chunks:
            yn = (cbuf[rows, c0:c0 + LANES] - mu) * rstd * lng_ref[:, c0:c0 + LANES] + lnb_ref[:, c0:c0 + LANES]
            y_ref[rows, d_seg + c0:d_seg + c0 + LANES] = (yn * jax.nn.sigmoid(yn)).astype(BF16)

    n_j, n_r = d_seg // tn, tm // rc
    for t in range(max(n_j, n_r)):
        if t < n_j:
            project(t)
        if t < n_r:
            convolve(t * rc)


def _mixer(hs, norm_g, w_in, b_in, cv_meta, z_meta, conv_sc_w, conv_cf_w, conv_cf_b, ln_g, ln_b,
           *, seq, tm, tn, name):
    m, d = hs.shape
    d_seg = w_in.shape[1] // 5
    rc = CONV_ROW_CHUNK
    assert m % tm == 0 and seq % tm == 0 and tm % rc == 0 and d_seg % tn == 0
    n_tiles = m // tm
    full = lambda shape: pl.BlockSpec(shape, lambda i: (0, 0))
    ring = (2, HALO + tm, d_seg)
    return pl.pallas_call(
        functools.partial(_mixer_body, n_tiles=n_tiles, tm=tm, tn=tn, tiles_per_seq=seq // tm, rc=rc),
        grid=(n_tiles + 1,),
        in_specs=[pl.BlockSpec((tm, d), lambda i: (jnp.minimum(i, n_tiles - 1), 0)),
                  full((1, d)),
                  pl.BlockSpec((d, 5 * d_seg), lambda i: (0, 0), pipeline_mode=pl.Buffered(1)),
                  full((1, 5 * d_seg)),
                  full((HALO, d_seg)), full((HALO, d_seg)),
                  full((SC_WIDTH, d_seg)), full((CF_WIDTH, d_seg)),
                  full((1, d_seg)), full((1, d_seg)), full((1, d_seg))],
        out_specs=pl.BlockSpec((tm, 2 * d_seg), lambda i: (jnp.maximum(i - 1, 0), 0)),
        out_shape=jax.ShapeDtypeStruct((m, 2 * d_seg), BF16),
        scratch_shapes=[pltpu.VMEM((tm, d), BF16),
                        pltpu.VMEM((2, tm, d_seg), F32),
                        pltpu.VMEM(ring, F32),
                        pltpu.VMEM(ring, F32),
                        pltpu.VMEM((tm, d_seg), F32)],
        compiler_params=pltpu.CompilerParams(
            dimension_semantics=("arbitrary",),
            vmem_limit_bytes=VMEM_LIMIT),
        name=name,
    )(hs, norm_g.reshape(1, d), w_in, b_in.reshape(1, 5 * d_seg), cv_meta, z_meta, conv_sc_w, conv_cf_w,
      conv_cf_b.reshape(1, d_seg), ln_g.reshape(1, d_seg), ln_b.reshape(1, d_seg))


def _outproj_body(hs_ref, y_ref, wo_ref, o_ref):
    o_ref[...] = hs_ref[...] + jnp.dot(y_ref[...], wo_ref[...], preferred_element_type=F32)


def _out_proj(hs, y, w_out, *, tm, name):
    m, d = hs.shape
    k = y.shape[1]
    assert m % tm == 0
    return pl.pallas_call(
        _outproj_body,
        grid=(m // tm,),
        in_specs=[pl.BlockSpec((tm, d), lambda i: (i, 0)),
                  pl.BlockSpec((tm, k), lambda i: (i, 0)),
                  pl.BlockSpec((k, d), lambda i: (0, 0), pipeline_mode=pl.Buffered(1))],
        out_specs=pl.BlockSpec((tm, d), lambda i: (i, 0)),
        out_shape=jax.ShapeDtypeStruct((m, d), F32),
        compiler_params=pltpu.CompilerParams(
            dimension_semantics=("arbitrary",),
            vmem_limit_bytes=VMEM_LIMIT),
        name=name,
    )(hs, y, w_out)


def kernel(x, meta_tokens, ffn1_norm, ffn1_w_gate, ffn1_w_up, ffn1_w_down, mix_norm, w_in, b_in,
           conv_sc_w, conv_cf_w, conv_cf_b, ln_cf_g, ln_cf_b, w_out,
           ffn2_norm, ffn2_w_gate, ffn2_w_up, ffn2_w_down, final_norm):
    bsz, seq, d = x.shape
    d_seg = conv_sc_w.shape[-1]
    assert ffn1_norm.shape[0] == 1 and meta_tokens.shape[0] == N_META
    hs = x.reshape(bsz * seq, d)
    hm = meta_tokens.astype(x.dtype)
    w1 = (ffn1_w_gate[0], ffn1_w_up[0], ffn1_w_down[0])
    w2 = (ffn2_w_gate[0], ffn2_w_up[0], ffn2_w_down[0])
    wi = w_in[0].astype(BF16)
    wo = w_out[0].astype(BF16)

    hs = _ffn(hs, ffn1_norm[0], *w1, tm=1024, tf=FFN_TF, name="ffn1")
    hm = _ffn(hm, ffn1_norm[0], *w1, tm=N_META, tf=FFN_TF, name="ffn1_meta")

    _, cvm, zm = _in_proj(hm, mix_norm[0], wi, b_in[0], tm=N_META, tn=256, name="in_proj_meta")
    pad = jnp.zeros((HALO - N_META, d_seg), F32)
    cv_halo = jnp.concatenate([pad, cvm], axis=0)
    z_halo = jnp.concatenate([pad, zm], axis=0)

    y = _mixer(hs, mix_norm[0], wi, b_in[0], cv_halo, z_halo, conv_sc_w[0], conv_cf_w[0], conv_cf_b[0],
               ln_cf_g[0], ln_cf_b[0], seq=seq, tm=512, tn=256, name="mixer")
    hs = _out_proj(hs, y, wo, tm=1024, name="out_proj")
    out = _ffn(hs, ffn2_norm[0], *w2, final_norm, tm=1024, tf=FFN_TF, name="ffn2_final")
    return out.reshape(bsz, seq, d)
```

```python
import functools

import jax
import jax.numpy as jnp
from jax import lax
from jax.experimental import pallas as pl
from jax.experimental.pallas import tpu as pltpu

N_META = 16
SC_WIDTH = 3
CF_WIDTH = 31
FFN_RES_SCALE = 0.5
EPS = 1e-6

SUBLANES = 8
LANES = 128
FFN_ROW_CHUNK = 256
FFN_TF = 512
CONV_ROW_CHUNK = 64
HALO = 32
V7X_VMEM_BYTES = 64 * 1024 * 1024
VMEM_LIMIT = V7X_VMEM_BYTES - 8 * 1024 * 1024

F32 = jnp.float32
BF16 = jnp.bfloat16


def _rmsnorm_f32(x, g):
    ms = jnp.mean(x * x, axis=-1, keepdims=True)
    return x * lax.rsqrt(ms + EPS) * g


def _ffn_body(x_hbm, g_ref, wg_hbm, wu_hbm, wd_hbm, *rest, n_tiles, n_f, tm, tf, rc, final, meta):
    rest = list(rest)
    fg_ref = rest.pop(0) if final else None
    xm_ref = rest.pop(0) if meta else None
    o_hbm = rest.pop(0)
    om_ref = rest.pop(0) if meta else None
    acc, xn_ref, wg_buf, wu_buf, wd_buf = rest[:5]
    rest = rest[5:]
    xnm_ref = rest.pop(0) if meta else None
    xsem, osem, wsem = rest
    i = pl.program_id(0)
    slot = i % 2

    def x_copy(tile, s):
        return pltpu.make_async_copy(x_hbm.at[pl.ds(tile * tm, tm), :], acc.at[s], xsem.at[s])

    def o_copy(tile, s):
        return pltpu.make_async_copy(acc.at[s], o_hbm.at[pl.ds(tile * tm, tm), :], osem.at[s])

    def w_copies(f, s):
        return (pltpu.make_async_copy(wg_hbm.at[:, pl.ds(f * tf, tf)], wg_buf.at[s], wsem.at[0, s]),
                pltpu.make_async_copy(wu_hbm.at[:, pl.ds(f * tf, tf)], wu_buf.at[s], wsem.at[1, s]),
                pltpu.make_async_copy(wd_hbm.at[pl.ds(f * tf, tf), :], wd_buf.at[s], wsem.at[2, s]))

    @pl.when(i == 0)
    def _():
        x_copy(0, 0).start()
        for c in w_copies(0, 0):
            c.start()
        if meta:
            xnm_ref[...] = _rmsnorm_f32(xm_ref[...], g_ref[...]).astype(BF16)
            om_ref[...] = xm_ref[...]

    x_copy(i, slot).wait()
    xn_ref[...] = _rmsnorm_f32(acc[slot], g_ref[...]).astype(BF16)

    def half_swiglu(xn, wslot):
        gate = jnp.dot(xn, wg_buf[wslot].astype(BF16), preferred_element_type=F32)
        up = jnp.dot(xn, wu_buf[wslot].astype(BF16), preferred_element_type=F32)
        h = (gate * jax.nn.sigmoid(gate) * up).astype(BF16)
        return FFN_RES_SCALE * jnp.dot(h, wd_buf[wslot].astype(BF16), preferred_element_type=F32)

    def f_step(f, carry):
        wslot = (i * n_f + f) % 2
        for c in w_copies(f, wslot):
            c.wait()

        @pl.when(jnp.logical_or(f + 1 < n_f, i + 1 < n_tiles))
        def _():
            for c in w_copies(jnp.where(f + 1 < n_f, f + 1, 0), 1 - wslot):
                c.start()

        @pl.when(jnp.logical_and(f == min(1, n_f - 1), i + 1 < n_tiles))
        def _():
            @pl.when(i >= 1)
            def _():
                o_copy(i - 1, 1 - slot).wait()
            x_copy(i + 1, 1 - slot).start()

        for r0 in range(0, tm, rc):
            acc[slot, r0:r0 + rc, :] += half_swiglu(xn_ref[r0:r0 + rc, :], wslot)

        if meta:
            @pl.when(i == 0)
            def _():
                om_ref[...] += half_swiglu(xnm_ref[...], wslot)
        return carry

    lax.fori_loop(0, n_f, f_step, 0)

    if final:
        acc[slot] = _rmsnorm_f32(acc[slot], fg_ref[...])
    o_copy(i, slot).start()

    @pl.when(i == n_tiles - 1)
    def _():
        if n_tiles > 1:
            o_copy(i - 1, 1 - slot).wait()
        o_copy(i, slot).wait()


def _ffn(x, norm_g, w_gate, w_up, w_down, final_g=None, x_meta=None, *, tm, tf, name):
    m, d = x.shape
    d_ff = w_gate.shape[1]
    assert m % tm == 0 and d_ff % tf == 0
    n_tiles, n_f = m // tm, d_ff // tf
    final, meta = final_g is not None, x_meta is not None
    any_spec = pl.BlockSpec(memory_space=pl.ANY)
    row_spec = pl.BlockSpec((1, d), lambda i: (0, 0))
    in_specs = [any_spec, row_spec, any_spec, any_spec, any_spec]
    args = [x, norm_g.reshape(1, d), w_gate, w_up, w_down]
    out_specs, out_shape = [any_spec], [jax.ShapeDtypeStruct((m, d), F32)]
    scratch = [pltpu.VMEM((2, tm, d), F32),
               pltpu.VMEM((tm, d), BF16),
               pltpu.VMEM((2, d, tf), w_gate.dtype),
               pltpu.VMEM((2, d, tf), w_up.dtype),
               pltpu.VMEM((2, tf, d), w_down.dtype)]
    if final:
        in_specs.append(row_spec)
        args.append(final_g.reshape(1, d))
    if meta:
        meta_spec = pl.BlockSpec(x_meta.shape, lambda i: (0, 0))
        in_specs.append(meta_spec)
        args.append(x_meta)
        out_specs.append(meta_spec)
        out_shape.append(jax.ShapeDtypeStruct(x_meta.shape, F32))
        scratch.append(pltpu.VMEM(x_meta.shape, BF16))
    scratch += [pltpu.SemaphoreType.DMA((2,)), pltpu.SemaphoreType.DMA((2,)), pltpu.SemaphoreType.DMA((3, 2))]
    outs = pl.pallas_call(
        functools.partial(_ffn_body, n_tiles=n_tiles, n_f=n_f, tm=tm, tf=tf,
                          rc=min(tm, FFN_ROW_CHUNK), final=final, meta=meta),
        grid=(n_tiles,),
        in_specs=in_specs,
        out_specs=out_specs,
        out_shape=out_shape,
        scratch_shapes=scratch,
        compiler_params=pltpu.CompilerParams(
            dimension_semantics=("arbitrary",),
            vmem_limit_bytes=VMEM_LIMIT),
        name=name,
    )(*args)
    return outs if meta else outs[0]


def _causal_taps(buf, w_ref, r0, c0, rc, width):
    base = HALO - (width - 1)
    out = None
    for s in range(SUBLANES):
        ks = [k for k in range(width) if (base + k) % SUBLANES == s]
        if not ks:
            continue
        rows = rc if s == 0 else rc + SUBLANES
        u = None
        for k in ks:
            a = base + k - s + r0
            t = buf[a:a + rows, c0:c0 + LANES] * w_ref[k:k + 1, c0:c0 + LANES]
            u = t if u is None else u + t
        if s:
            u = u[s:s + rc]
        out = u if out is None else out + u
    return out


def _mixer_body(x_ref, xm_ref, g_ref, win_ref, bias_ref, wsc_ref, wcf_ref, bcf_ref, lng_ref, lnb_ref,
                y_ref, xn_ref, bring, cvring, zring, cvhalo, zhalo, cbuf,
                *, n_tiles, tm, tn, tiles_per_seq, rc):
    i = pl.program_id(0)
    sp = i % 2
    sc = 1 - sp
    d_seg = zring.shape[2]
    n_meta = xm_ref.shape[0]

    @pl.when(i == 0)
    def _():
        bring[1] = jnp.zeros(bring.shape[1:], F32)
        cvring[1] = jnp.zeros(cvring.shape[1:], F32)
        zring[1] = jnp.zeros(zring.shape[1:], F32)

        xm = _rmsnorm_f32(xm_ref[...], g_ref[...]).astype(BF16)

        def proj_meta(s):
            cols = slice(s * d_seg, (s + 1) * d_seg)
            return jnp.dot(xm, win_ref[:, cols], preferred_element_type=F32) + bias_ref[:, cols]
        cvhalo[0:HALO - n_meta, :] = jnp.zeros((HALO - n_meta, d_seg), F32)
        zhalo[0:HALO - n_meta, :] = jnp.zeros((HALO - n_meta, d_seg), F32)
        cvhalo[HALO - n_meta:, :] = proj_meta(1) * proj_meta(2)
        zhalo[HALO - n_meta:, :] = proj_meta(3) * jax.nn.sigmoid(proj_meta(4))

    first = jnp.minimum(i, n_tiles - 1) % tiles_per_seq == 0
    cvring[sp, 0:HALO, :] = jnp.where(first, cvhalo[...], cvring[sc, tm:tm + HALO, :])
    zring[sp, 0:HALO, :] = jnp.where(first, zhalo[...], zring[sc, tm:tm + HALO, :])

    xn_ref[...] = _rmsnorm_f32(x_ref[...], g_ref[...]).astype(BF16)

    def project(j):
        def proj(s):
            cols = slice(s * d_seg + j * tn, s * d_seg + (j + 1) * tn)
            return jnp.dot(xn_ref[...], win_ref[:, cols], preferred_element_type=F32) + bias_ref[:, cols]
        cols = slice(j * tn, (j + 1) * tn)
        bring[sp, :, cols] = proj(0)
        cvring[sp, HALO:, cols] = proj(1) * proj(2)
        zring[sp, HALO:, cols] = proj(3) * jax.nn.sigmoid(proj(4))

    cvbuf, zbuf, bbuf = cvring.at[sc], zring.at[sc], bring.at[sc]
    inv_n = 1.0 / d_seg
    lane_chunks = range(0, d_seg, LANES)

    def convolve(r0):
        rows = slice(r0, r0 + rc)
        for c0 in lane_chunks:
            acc = _causal_taps(cvbuf, wsc_ref, r0, c0, rc, SC_WIDTH)
            y_ref[rows, c0:c0 + LANES] = (bbuf[rows, c0:c0 + LANES] * acc).astype(BF16)

        s1 = jnp.zeros((rc, LANES), F32)
        for c0 in lane_chunks:
            acc = _causal_taps(zbuf, wcf_ref, r0, c0, rc, CF_WIDTH) + bcf_ref[:, c0:c0 + LANES]
            cbuf[rows, c0:c0 + LANES] = acc
            s1 = s1 + acc
        mu = jnp.sum(s1, axis=-1, keepdims=True) * inv_n
        s2 = jnp.zeros((rc, LANES), F32)
        for c0 in lane_chunks:
            dlt = cbuf[rows, c0:c0 + LANES] - mu
            s2 = s2 + dlt * dlt
        var = jnp.sum(s2, axis=-1, keepdims=True) * inv_n
        rstd = lax.rsqrt(var + EPS)
        for c0 in lane_chunks:
            yn = (cbuf[rows, c0:c0 + LANES] - mu) * rstd * lng_ref[:, c0:c0 + LANES] + lnb_ref[:, c0:c0 + LANES]
            y_ref[rows, d_seg + c0:d_seg + c0 + LANES] = (yn * jax.nn.sigmoid(yn)).astype(BF16)

    n_j, n_r = d_seg // tn, tm // rc
    for t in range(max(n_j, n_r)):
        if t < n_j:
            project(t)
        if t < n_r:
            convolve(t * rc)


def _mixer(hs, hs_meta, norm_g, w_in, b_in, conv_sc_w, conv_cf_w, conv_cf_b, ln_g, ln_b,
           *, seq, tm, tn, name):
    m, d = hs.shape
    d_seg = w_in.shape[1] // 5
    n_meta = hs_meta.shape[0]
    rc = CONV_ROW_CHUNK
    assert m % tm == 0 and seq % tm == 0 and tm % rc == 0 and d_seg % tn == 0
    assert n_meta <= HALO and (HALO - n_meta) % SUBLANES == 0
    n_tiles = m // tm
    full = lambda shape: pl.BlockSpec(shape, lambda i: (0, 0))
    ring = (2, HALO + tm, d_seg)
    return pl.pallas_call(
        functools.partial(_mixer_body, n_tiles=n_tiles, tm=tm, tn=tn, tiles_per_seq=seq // tm, rc=rc),
        grid=(n_tiles + 1,),
        in_specs=[pl.BlockSpec((tm, d), lambda i: (jnp.minimum(i, n_tiles - 1), 0)),
                  full((n_meta, d)),
                  full((1, d)),
                  pl.BlockSpec((d, 5 * d_seg), lambda i: (0, 0), pipeline_mode=pl.Buffered(1)),
                  full((1, 5 * d_seg)),
                  full((SC_WIDTH, d_seg)), full((CF_WIDTH, d_seg)),
                  full((1, d_seg)), full((1, d_seg)), full((1, d_seg))],
        out_specs=pl.BlockSpec((tm, 2 * d_seg), lambda i: (jnp.maximum(i - 1, 0), 0)),
        out_shape=jax.ShapeDtypeStruct((m, 2 * d_seg), BF16),
        scratch_shapes=[pltpu.VMEM((tm, d), BF16),
                        pltpu.VMEM((2, tm, d_seg), F32),
                        pltpu.VMEM(ring, F32),
                        pltpu.VMEM(ring, F32),
                        pltpu.VMEM((HALO, d_seg), F32),
                        pltpu.VMEM((HALO, d_seg), F32),
                        pltpu.VMEM((tm, d_seg), F32)],
        compiler_params=pltpu.CompilerParams(
            dimension_semantics=("arbitrary",),
            vmem_limit_bytes=VMEM_LIMIT),
        name=name,
    )(hs, hs_meta, norm_g.reshape(1, d), w_in, b_in.reshape(1, 5 * d_seg), conv_sc_w, conv_cf_w,
      conv_cf_b.reshape(1, d_seg), ln_g.reshape(1, d_seg), ln_b.reshape(1, d_seg))


def _outproj_body(hs_ref, y_ref, wo_ref, o_ref):
    o_ref[...] = hs_ref[...] + jnp.dot(y_ref[...], wo_ref[...], preferred_element_type=F32)


def _out_proj(hs, y, w_out, *, tm, name):
    m, d = hs.shape
    k = y.shape[1]
    assert m % tm == 0
    return pl.pallas_call(
        _outproj_body,
        grid=(m // tm,),
        in_specs=[pl.BlockSpec((tm, d), lambda i: (i, 0)),
                  pl.BlockSpec((tm, k), lambda i: (i, 0)),
                  pl.BlockSpec((k, d), lambda i: (0, 0), pipeline_mode=pl.Buffered(1))],
        out_specs=pl.BlockSpec((tm, d), lambda i: (i, 0)),
        out_shape=jax.ShapeDtypeStruct((m, d), F32),
        compiler_params=pltpu.CompilerParams(
            dimension_semantics=("arbitrary",),
            vmem_limit_bytes=VMEM_LIMIT),
        name=name,
    )(hs, y, w_out)


def kernel(x, meta_tokens, ffn1_norm, ffn1_w_gate, ffn1_w_up, ffn1_w_down, mix_norm, w_in, b_in,
           conv_sc_w, conv_cf_w, conv_cf_b, ln_cf_g, ln_cf_b, w_out,
           ffn2_norm, ffn2_w_gate, ffn2_w_up, ffn2_w_down, final_norm):
    bsz, seq, d = x.shape
    assert ffn1_norm.shape[0] == 1 and meta_tokens.shape[0] == N_META
    hs = x.reshape(bsz * seq, d)
    hm = meta_tokens.astype(x.dtype)
    w1 = (ffn1_w_gate[0], ffn1_w_up[0], ffn1_w_down[0])
    w2 = (ffn2_w_gate[0], ffn2_w_up[0], ffn2_w_down[0])
    wi = w_in[0].astype(BF16)
    wo = w_out[0].astype(BF16)

    hs, hm = _ffn(hs, ffn1_norm[0], *w1, x_meta=hm, tm=1024, tf=FFN_TF, name="ffn1")

    y = _mixer(hs, hm, mix_norm[0], wi, b_in[0], conv_sc_w[0], conv_cf_w[0], conv_cf_b[0],
               ln_cf_g[0], ln_cf_b[0], seq=seq, tm=512, tn=256, name="mixer")
    hs = _out_proj(hs, y, wo, tm=1024, name="out_proj")
    out = _ffn(hs, ffn2_norm[0], *w2, final_norm, tm=1024, tf=FFN_TF, name="ffn2_final")
    return out.reshape(bsz, seq, d)
```

```python
import functools

import jax
import jax.numpy as jnp
from jax import lax
from jax.experimental import pallas as pl
from jax.experimental.pallas import tpu as pltpu

N_META = 16
SC_WIDTH = 3
CF_WIDTH = 31
FFN_RES_SCALE = 0.5
EPS = 1e-6

SUBLANES = 8
LANES = 128
FFN_TF = 512
CONV_ROW_CHUNK = 64
HALO = 32
V7X_VMEM_BYTES = 64 * 1024 * 1024
VMEM_LIMIT = V7X_VMEM_BYTES - 8 * 1024 * 1024

F32 = jnp.float32
BF16 = jnp.bfloat16


def _rmsnorm_f32(x, g):
    ms = jnp.mean(x * x, axis=-1, keepdims=True)
    return x * lax.rsqrt(ms + EPS) * g


def _ffn_body(x_hbm, g_ref, wg_hbm, wu_hbm, wd_hbm, *rest, n_tiles, n_f, tm, tf, final, meta):
    rest = list(rest)
    fg_ref = rest.pop(0) if final else None
    xm_ref = rest.pop(0) if meta else None
    o_hbm = rest.pop(0)
    om_ref = rest.pop(0) if meta else None
    acc, xn_ref, wg_buf, wu_buf, wd_buf = rest[:5]
    rest = rest[5:]
    xnm_ref = rest.pop(0) if meta else None
    xsem, osem, wsem = rest
    i = pl.program_id(0)
    slot = i % 2

    def x_copy(tile, s):
        return pltpu.make_async_copy(x_hbm.at[pl.ds(tile * tm, tm), :], acc.at[s], xsem.at[s])

    def o_copy(tile, s):
        return pltpu.make_async_copy(acc.at[s], o_hbm.at[pl.ds(tile * tm, tm), :], osem.at[s])

    def w_copies(f, s):
        return (pltpu.make_async_copy(wg_hbm.at[:, pl.ds(f * tf, tf)], wg_buf.at[s], wsem.at[0, s]),
                pltpu.make_async_copy(wu_hbm.at[:, pl.ds(f * tf, tf)], wu_buf.at[s], wsem.at[1, s]),
                pltpu.make_async_copy(wd_hbm.at[pl.ds(f * tf, tf), :], wd_buf.at[s], wsem.at[2, s]))

    @pl.when(i == 0)
    def _():
        x_copy(0, 0).start()
        for c in w_copies(0, 0):
            c.start()
        if meta:
            xnm_ref[...] = _rmsnorm_f32(xm_ref[...], g_ref[...]).astype(BF16)
            om_ref[...] = xm_ref[...]

    x_copy(i, slot).wait()
    xn_ref[...] = _rmsnorm_f32(acc[slot], g_ref[...]).astype(BF16)

    def half_swiglu(xn, wslot):
        gate = jnp.dot(xn, wg_buf[wslot].astype(BF16), preferred_element_type=F32)
        up = jnp.dot(xn, wu_buf[wslot].astype(BF16), preferred_element_type=F32)
        h = (gate * jax.nn.sigmoid(gate) * up).astype(BF16)
        return FFN_RES_SCALE * jnp.dot(h, wd_buf[wslot].astype(BF16), preferred_element_type=F32)

    def f_step(f, carry):
        wslot = (i * n_f + f) % 2
        for c in w_copies(f, wslot):
            c.wait()

        @pl.when(jnp.logical_or(f + 1 < n_f, i + 1 < n_tiles))
        def _():
            for c in w_copies(jnp.where(f + 1 < n_f, f + 1, 0), 1 - wslot):
                c.start()

        @pl.when(jnp.logical_and(f == min(1, n_f - 1), i + 1 < n_tiles))
        def _():
            @pl.when(i >= 1)
            def _():
                o_copy(i - 1, 1 - slot).wait()
            x_copy(i + 1, 1 - slot).start()

        acc[slot] += half_swiglu(xn_ref[...], wslot)

        if meta:
            @pl.when(i == 0)
            def _():
                om_ref[...] += half_swiglu(xnm_ref[...], wslot)
        return carry

    lax.fori_loop(0, n_f, f_step, 0)

    if final:
        acc[slot] = _rmsnorm_f32(acc[slot], fg_ref[...])
    o_copy(i, slot).start()

    @pl.when(i == n_tiles - 1)
    def _():
        if n_tiles > 1:
            o_copy(i - 1, 1 - slot).wait()
        o_copy(i, slot).wait()


def _ffn(x, norm_g, w_gate, w_up, w_down, final_g=None, x_meta=None, *, tm, tf, name):
    m, d = x.shape
    d_ff = w_gate.shape[1]
    assert m % tm == 0 and d_ff % tf == 0
    n_tiles, n_f = m // tm, d_ff // tf
    final, meta = final_g is not None, x_meta is not None
    any_spec = pl.BlockSpec(memory_space=pl.ANY)
    row_spec = pl.BlockSpec((1, d), lambda i: (0, 0))
    in_specs = [any_spec, row_spec, any_spec, any_spec, any_spec]
    args = [x, norm_g.reshape(1, d), w_gate, w_up, w_down]
    out_specs, out_shape = [any_spec], [jax.ShapeDtypeStruct((m, d), F32)]
    scratch = [pltpu.VMEM((2, tm, d), F32),
               pltpu.VMEM((tm, d), BF16),
               pltpu.VMEM((2, d, tf), w_gate.dtype),
               pltpu.VMEM((2, d, tf), w_up.dtype),
               pltpu.VMEM((2, tf, d), w_down.dtype)]
    if final:
        in_specs.append(row_spec)
        args.append(final_g.reshape(1, d))
    if meta:
        meta_spec = pl.BlockSpec(x_meta.shape, lambda i: (0, 0))
        in_specs.append(meta_spec)
        args.append(x_meta)
        out_specs.append(meta_spec)
        out_shape.append(jax.ShapeDtypeStruct(x_meta.shape, F32))
        scratch.append(pltpu.VMEM(x_meta.shape, BF16))
    scratch += [pltpu.SemaphoreType.DMA((2,)), pltpu.SemaphoreType.DMA((2,)), pltpu.SemaphoreType.DMA((3, 2))]
    outs = pl.pallas_call(
        functools.partial(_ffn_body, n_tiles=n_tiles, n_f=n_f, tm=tm, tf=tf, final=final, meta=meta),
        grid=(n_tiles,),
        in_specs=in_specs,
        out_specs=out_specs,
        out_shape=out_shape,
        scratch_shapes=scratch,
        compiler_params=pltpu.CompilerParams(
            dimension_semantics=("arbitrary",),
            vmem_limit_bytes=VMEM_LIMIT),
        name=name,
    )(*args)
    return outs if meta else outs[0]


def _causal_taps(buf, w_ref, r0, c0, rc, width):
    base = HALO - (width - 1)
    out = None
    for s in range(SUBLANES):
        ks = [k for k in range(width) if (base + k) % SUBLANES == s]
        if not ks:
            continue
        rows = rc if s == 0 else rc + SUBLANES
        u = None
        for k in ks:
            a = base + k - s + r0
            t = buf[a:a + rows, c0:c0 + LANES] * w_ref[k:k + 1, c0:c0 + LANES]
            u = t if u is None else u + t
        if s:
            u = u[s:s + rc]
        out = u if out is None else out + u
    return out


def _mixer_body(x_ref, xm_ref, g_ref, win_ref, bias_ref, wsc_ref, wcf_ref, bcf_ref, lng_ref, lnb_ref,
                y_ref, xn_ref, bring, cvring, zring, cvhalo, zhalo, cbuf,
                *, n_tiles, tm, tn, tiles_per_seq, rc):
    i = pl.program_id(0)
    sp = i % 2
    sc = 1 - sp
    d_seg = zring.shape[2]
    n_meta = xm_ref.shape[0]

    @pl.when(i == 0)
    def _():
        bring[1] = jnp.zeros(bring.shape[1:], F32)
        cvring[1] = jnp.zeros(cvring.shape[1:], F32)
        zring[1] = jnp.zeros(zring.shape[1:], F32)

        xm = _rmsnorm_f32(xm_ref[...], g_ref[...]).astype(BF16)

        def proj_meta(s):
            cols = slice(s * d_seg, (s + 1) * d_seg)
            return jnp.dot(xm, win_ref[:, cols], preferred_element_type=F32) + bias_ref[:, cols]
        cvhalo[0:HALO - n_meta, :] = jnp.zeros((HALO - n_meta, d_seg), F32)
        zhalo[0:HALO - n_meta, :] = jnp.zeros((HALO - n_meta, d_seg), F32)
        cvhalo[HALO - n_meta:, :] = proj_meta(1) * proj_meta(2)
        zhalo[HALO - n_meta:, :] = proj_meta(3) * jax.nn.sigmoid(proj_meta(4))

    first = jnp.minimum(i, n_tiles - 1) % tiles_per_seq == 0
    cvring[sp, 0:HALO, :] = jnp.where(first, cvhalo[...], cvring[sc, tm:tm + HALO, :])
    zring[sp, 0:HALO, :] = jnp.where(first, zhalo[...], zring[sc, tm:tm + HALO, :])

    xn_ref[...] = _rmsnorm_f32(x_ref[...], g_ref[...]).astype(BF16)

    def project(j):
        def proj(s):
            cols = slice(s * d_seg + j * tn, s * d_seg + (j + 1) * tn)
            return jnp.dot(xn_ref[...], win_ref[:, cols], preferred_element_type=F32) + bias_ref[:, cols]
        cols = slice(j * tn, (j + 1) * tn)
        bring[sp, :, cols] = proj(0)
        cvring[sp, HALO:, cols] = proj(1) * proj(2)
        zring[sp, HALO:, cols] = proj(3) * jax.nn.sigmoid(proj(4))

    cvbuf, zbuf, bbuf = cvring.at[sc], zring.at[sc], bring.at[sc]
    inv_n = 1.0 / d_seg
    lane_chunks = range(0, d_seg, LANES)

    def convolve(r0):
        rows = slice(r0, r0 + rc)
        for c0 in lane_chunks:
            acc = _causal_taps(cvbuf, wsc_ref, r0, c0, rc, SC_WIDTH)
            y_ref[rows, c0:c0 + LANES] = (bbuf[rows, c0:c0 + LANES] * acc).astype(BF16)

        s1 = jnp.zeros((rc, LANES), F32)
        for c0 in lane_chunks:
            acc = _causal_taps(zbuf, wcf_ref, r0, c0, rc, CF_WIDTH) + bcf_ref[:, c0:c0 + LANES]
            cbuf[rows, c0:c0 + LANES] = acc
            s1 = s1 + acc
        mu = jnp.sum(s1, axis=-1, keepdims=True) * inv_n
        s2 = jnp.zeros((rc, LANES), F32)
        for c0 in lane_chunks:
            dlt = cbuf[rows, c0:c0 + LANES] - mu
            s2 = s2 + dlt * dlt
        var = jnp.sum(s2, axis=-1, keepdims=True) * inv_n
        rstd = lax.rsqrt(var + EPS)
        for c0 in lane_chunks:
            yn = (cbuf[rows, c0:c0 + LANES] - mu) * rstd * lng_ref[:, c0:c0 + LANES] + lnb_ref[:, c0:c0 + LANES]
            y_ref[rows, d_seg + c0:d_seg + c0 + LANES] = (yn * jax.nn.sigmoid(yn)).astype(BF16)

    n_j, n_r = d_seg // tn, tm // rc
    for t in range(max(n_j, n_r)):
        if t < n_j:
            project(t)
        if t < n_r:
            convolve(t * rc)


def _mixer(hs, hs_meta, norm_g, w_in, b_in, conv_sc_w, conv_cf_w, conv_cf_b, ln_g, ln_b,
           *, seq, tm, tn, name):
    m, d = hs.shape
    d_seg = w_in.shape[1] // 5
    n_meta = hs_meta.shape[0]
    rc = CONV_ROW_CHUNK
    assert m % tm == 0 and seq % tm == 0 and tm % rc == 0 and d_seg % tn == 0
    assert n_meta <= HALO and (HALO - n_meta) % SUBLANES == 0
    n_tiles = m // tm
    full = lambda shape: pl.BlockSpec(shape, lambda i: (0, 0))
    ring = (2, HALO + tm, d_seg)
    return pl.pallas_call(
        functools.partial(_mixer_body, n_tiles=n_tiles, tm=tm, tn=tn, tiles_per_seq=seq // tm, rc=rc),
        grid=(n_tiles + 1,),
        in_specs=[pl.BlockSpec((tm, d), lambda i: (jnp.minimum(i, n_tiles - 1), 0)),
                  full((n_meta, d)),
                  full((1, d)),
                  pl.BlockSpec((d, 5 * d_seg), lambda i: (0, 0), pipeline_mode=pl.Buffered(1)),
                  full((1, 5 * d_seg)),
                  full((SC_WIDTH, d_seg)), full((CF_WIDTH, d_seg)),
                  full((1, d_seg)), full((1, d_seg)), full((1, d_seg))],
        out_specs=pl.BlockSpec((tm, 2 * d_seg), lambda i: (jnp.maximum(i - 1, 0), 0)),
        out_shape=jax.ShapeDtypeStruct((m, 2 * d_seg), BF16),
        scratch_shapes=[pltpu.VMEM((tm, d), BF16),
                        pltpu.VMEM((2, tm, d_seg), F32),
                        pltpu.VMEM(ring, F32),
                        pltpu.VMEM(ring, F32),
                        pltpu.VMEM((HALO, d_seg), F32),
                        pltpu.VMEM((HALO, d_seg), F32),
                        pltpu.VMEM((tm, d_seg), F32)],
        compiler_params=pltpu.CompilerParams(
            dimension_semantics=("arbitrary",),
            vmem_limit_bytes=VMEM_LIMIT),
        name=name,
    )(hs, hs_meta, norm_g.reshape(1, d), w_in, b_in.reshape(1, 5 * d_seg), conv_sc_w, conv_cf_w,
      conv_cf_b.reshape(1, d_seg), ln_g.reshape(1, d_seg), ln_b.reshape(1, d_seg))


def _outproj_body(hs_ref, y_ref, wo_ref, o_ref):
    o_ref[...] = hs_ref[...] + jnp.dot(y_ref[...], wo_ref[...], preferred_element_type=F32)


def _out_proj(hs, y, w_out, *, tm, name):
    m, d = hs.shape
    k = y.shape[1]
    assert m % tm == 0
    return pl.pallas_call(
        _outproj_body,
        grid=(m // tm,),
        in_specs=[pl.BlockSpec((tm, d), lambda i: (i, 0)),
                  pl.BlockSpec((tm, k), lambda i: (i, 0)),
                  pl.BlockSpec((k, d), lambda i: (0, 0), pipeline_mode=pl.Buffered(1))],
        out_specs=pl.BlockSpec((tm, d), lambda i: (i, 0)),
        out_shape=jax.ShapeDtypeStruct((m, d), F32),
        compiler_params=pltpu.CompilerParams(
            dimension_semantics=("arbitrary",),
            vmem_limit_bytes=VMEM_LIMIT),
        name=name,
    )(hs, y, w_out)


def kernel(x, meta_tokens, ffn1_norm, ffn1_w_gate, ffn1_w_up, ffn1_w_down, mix_norm, w_in, b_in,
           conv_sc_w, conv_cf_w, conv_cf_b, ln_cf_g, ln_cf_b, w_out,
           ffn2_norm, ffn2_w_gate, ffn2_w_up, ffn2_w_down, final_norm):
    bsz, seq, d = x.shape
    assert ffn1_norm.shape[0] == 1 and meta_tokens.shape[0] == N_META
    hs = x.reshape(bsz * seq, d)
    hm = meta_tokens.astype(x.dtype)
    w1 = (ffn1_w_gate[0], ffn1_w_up[0], ffn1_w_down[0])
    w2 = (ffn2_w_gate[0], ffn2_w_up[0], ffn2_w_down[0])
    wi = w_in[0].astype(BF16)
    wo = w_out[0].astype(BF16)

    hs, hm = _ffn(hs, ffn1_norm[0], *w1, x_meta=hm, tm=1024, tf=FFN_TF, name="ffn1")

    y = _mixer(hs, hm, mix_norm[0], wi, b_in[0], conv_sc_w[0], conv_cf_w[0], conv_cf_b[0],
               ln_cf_g[0], ln_cf_b[0], seq=seq, tm=512, tn=256, name="mixer")
    hs = _out_proj(hs, y, wo, tm=1024, name="out_proj")
    out = _ffn(hs, ffn2_norm[0], *w2, final_norm, tm=1024, tf=FFN_TF, name="ffn2_final")
    return out.reshape(bsz, seq, d)
```

```python
import functools

import jax
import jax.numpy as jnp
from jax import lax
from jax.experimental import pallas as pl
from jax.experimental.pallas import tpu as pltpu

N_META = 16
SC_WIDTH = 3
CF_WIDTH = 31
FFN_RES_SCALE = 0.5
EPS = 1e-6

SUBLANES = 8
LANES = 128
FFN_TF = 512
CONV_ROW_CHUNK = 64
HALO = 32
V7X_VMEM_BYTES = 64 * 1024 * 1024
VMEM_LIMIT = V7X_VMEM_BYTES - 8 * 1024 * 1024

F32 = jnp.float32
BF16 = jnp.bfloat16


def _rmsnorm_f32(x, g):
    ms = jnp.mean(x * x, axis=-1, keepdims=True)
    return x * lax.rsqrt(ms + EPS) * g


def _ffn_body(x_hbm, g_ref, wg_hbm, wu_hbm, wd_hbm, *rest, n_tiles, n_f, tm, tf, final, meta):
    rest = list(rest)
    fg_ref = rest.pop(0) if final else None
    xm_ref = rest.pop(0) if meta else None
    o_hbm = rest.pop(0)
    om_ref = rest.pop(0) if meta else None
    acc, xn_ref, wg_buf, wu_buf, wd_buf = rest[:5]
    rest = rest[5:]
    xnm_ref = rest.pop(0) if meta else None
    xsem, osem, wsem = rest
    i = pl.program_id(0)
    slot = i % 2

    def x_copy(tile, s):
        return pltpu.make_async_copy(x_hbm.at[pl.ds(tile * tm, tm), :], acc.at[s], xsem.at[s])

    def o_copy(tile, s):
        return pltpu.make_async_copy(acc.at[s], o_hbm.at[pl.ds(tile * tm, tm), :], osem.at[s])

    def w_copies(f, s):
        return (pltpu.make_async_copy(wg_hbm.at[:, pl.ds(f * tf, tf)], wg_buf.at[s], wsem.at[0, s]),
                pltpu.make_async_copy(wu_hbm.at[:, pl.ds(f * tf, tf)], wu_buf.at[s], wsem.at[1, s]),
                pltpu.make_async_copy(wd_hbm.at[pl.ds(f * tf, tf), :], wd_buf.at[s], wsem.at[2, s]))

    @pl.when(i == 0)
    def _():
        x_copy(0, 0).start()
        for c in w_copies(0, 0):
            c.start()
        if meta:
            xnm_ref[...] = _rmsnorm_f32(xm_ref[...], g_ref[...]).astype(BF16)
            om_ref[...] = xm_ref[...]

    x_copy(i, slot).wait()
    xn_ref[...] = _rmsnorm_f32(acc[slot], g_ref[...]).astype(BF16)

    def half_swiglu(xn, wslot):
        gate = jnp.dot(xn, wg_buf[wslot].astype(BF16), preferred_element_type=F32)
        up = jnp.dot(xn, wu_buf[wslot].astype(BF16), preferred_element_type=F32)
        h = (gate * jax.nn.sigmoid(gate) * up).astype(BF16)
        return FFN_RES_SCALE * jnp.dot(h, wd_buf[wslot].astype(BF16), preferred_element_type=F32)

    def f_step(f, carry):
        wslot = (i * n_f + f) % 2
        for c in w_copies(f, wslot):
            c.wait()

        @pl.when(jnp.logical_or(f + 1 < n_f, i + 1 < n_tiles))
        def _():
            for c in w_copies(jnp.where(f + 1 < n_f, f + 1, 0), 1 - wslot):
                c.start()

        @pl.when(jnp.logical_and(f == min(1, n_f - 1), i + 1 < n_tiles))
        def _():
            @pl.when(i >= 1)
            def _():
                o_copy(i - 1, 1 - slot).wait()
            x_copy(i + 1, 1 - slot).start()

        acc[slot] += half_swiglu(xn_ref[...], wslot)

        if meta:
            @pl.when(i == 0)
            def _():
                om_ref[...] += half_swiglu(xnm_ref[...], wslot)
        return carry

    lax.fori_loop(0, n_f, f_step, 0)

    if final:
        acc[slot] = _rmsnorm_f32(acc[slot], fg_ref[...])
    o_copy(i, slot).start()

    @pl.when(i == n_tiles - 1)
    def _():
        if n_tiles > 1:
            o_copy(i - 1, 1 - slot).wait()
        o_copy(i, slot).wait()


def _ffn(x, norm_g, w_gate, w_up, w_down, final_g=None, x_meta=None, *, tm, tf, name):
    m, d = x.shape
    d_ff = w_gate.shape[1]
    assert m % tm == 0 and d_ff % tf == 0
    n_tiles, n_f = m // tm, d_ff // tf
    final, meta = final_g is not None, x_meta is not None
    any_spec = pl.BlockSpec(memory_space=pl.ANY)
    row_spec = pl.BlockSpec((1, d), lambda i: (0, 0))
    in_specs = [any_spec, row_spec, any_spec, any_spec, any_spec]
    args = [x, norm_g.reshape(1, d), w_gate, w_up, w_down]
    out_specs, out_shape = [any_spec], [jax.ShapeDtypeStruct((m, d), F32)]
    scratch = [pltpu.VMEM((2, tm, d), F32),
               pltpu.VMEM((tm, d), BF16),
               pltpu.VMEM((2, d, tf), w_gate.dtype),
               pltpu.VMEM((2, d, tf), w_up.dtype),
               pltpu.VMEM((2, tf, d), w_down.dtype)]
    if final:
        in_specs.append(row_spec)
        args.append(final_g.reshape(1, d))
    if meta:
        meta_spec = pl.BlockSpec(x_meta.shape, lambda i: (0, 0))
        in_specs.append(meta_spec)
        args.append(x_meta)
        out_specs.append(meta_spec)
        out_shape.append(jax.ShapeDtypeStruct(x_meta.shape, F32))
        scratch.append(pltpu.VMEM(x_meta.shape, BF16))
    scratch += [pltpu.SemaphoreType.DMA((2,)), pltpu.SemaphoreType.DMA((2,)), pltpu.SemaphoreType.DMA((3, 2))]
    outs = pl.pallas_call(
        functools.partial(_ffn_body, n_tiles=n_tiles, n_f=n_f, tm=tm, tf=tf, final=final, meta=meta),
        grid=(n_tiles,),
        in_specs=in_specs,
        out_specs=out_specs,
        out_shape=out_shape,
        scratch_shapes=scratch,
        compiler_params=pltpu.CompilerParams(
            dimension_semantics=("arbitrary",),
            vmem_limit_bytes=VMEM_LIMIT),
        name=name,
    )(*args)
    return outs if meta else outs[0]


def _causal_taps(buf, w_ref, r0, c0, rc, width):
    base = HALO - (width - 1)
    out = None
    for s in range(SUBLANES):
        ks = [k for k in range(width) if (base + k) % SUBLANES == s]
        if not ks:
            continue
        rows = rc if s == 0 else rc + SUBLANES
        u = None
        for k in ks:
            a = base + k - s + r0
            t = buf[a:a + rows, c0:c0 + LANES] * w_ref[k:k + 1, c0:c0 + LANES]
            u = t if u is None else u + t
        if s:
            u = u[s:s + rc]
        out = u if out is None else out + u
    return out


def _mixer_body(x_ref, xm_ref, g_ref, win_ref, bias_ref, wsc_ref, wcf_ref, bcf_ref, lng_ref, lnb_ref,
                y_ref, xn_ref, bring, cvring, zring, cvhalo, zhalo, cbuf,
                *, n_tiles, tm, tn, tiles_per_seq, rc):
    i = pl.program_id(0)
    sp = i % 2
    sc = 1 - sp
    d_seg = zring.shape[2]
    n_meta = xm_ref.shape[0]

    @pl.when(i == 0)
    def _():
        bring[1] = jnp.zeros(bring.shape[1:], F32)
        cvring[1] = jnp.zeros(cvring.shape[1:], F32)
        zring[1] = jnp.zeros(zring.shape[1:], F32)

        xm = _rmsnorm_f32(xm_ref[...], g_ref[...]).astype(BF16)

        def proj_meta(s):
            cols = slice(s * d_seg, (s + 1) * d_seg)
            return jnp.dot(xm, win_ref[:, cols], preferred_element_type=F32) + bias_ref[:, cols]
        cvhalo[0:HALO - n_meta, :] = jnp.zeros((HALO - n_meta, d_seg), F32)
        zhalo[0:HALO - n_meta, :] = jnp.zeros((HALO - n_meta, d_seg), F32)
        cvhalo[HALO - n_meta:, :] = proj_meta(1) * proj_meta(2)
        zhalo[HALO - n_meta:, :] = proj_meta(3) * jax.nn.sigmoid(proj_meta(4))

    first = jnp.minimum(i, n_tiles - 1) % tiles_per_seq == 0
    cvring[sp, 0:HALO, :] = jnp.where(first, cvhalo[...], cvring[sc, tm:tm + HALO, :])
    zring[sp, 0:HALO, :] = jnp.where(first, zhalo[...], zring[sc, tm:tm + HALO, :])

    xn_ref[...] = _rmsnorm_f32(x_ref[...], g_ref[...]).astype(BF16)

    def project(j):
        def proj(s):
            cols = slice(s * d_seg + j * tn, s * d_seg + (j + 1) * tn)
            return jnp.dot(xn_ref[...], win_ref[:, cols], preferred_element_type=F32) + bias_ref[:, cols]
        cols = slice(j * tn, (j + 1) * tn)
        bring[sp, :, cols] = proj(0)
        cvring[sp, HALO:, cols] = proj(1) * proj(2)
        zring[sp, HALO:, cols] = proj(3) * jax.nn.sigmoid(proj(4))

    cvbuf, zbuf, bbuf = cvring.at[sc], zring.at[sc], bring.at[sc]
    inv_n = 1.0 / d_seg
    lane_chunks = range(0, d_seg, LANES)

    def convolve(r0):
        rows = slice(r0, r0 + rc)
        for c0 in lane_chunks:
            acc = _causal_taps(cvbuf, wsc_ref, r0, c0, rc, SC_WIDTH)
            y_ref[rows, c0:c0 + LANES] = (bbuf[rows, c0:c0 + LANES] * acc).astype(BF16)

        s1 = jnp.zeros((rc, LANES), F32)
        for c0 in lane_chunks:
            acc = _causal_taps(zbuf, wcf_ref, r0, c0, rc, CF_WIDTH) + bcf_ref[:, c0:c0 + LANES]
            cbuf[rows, c0:c0 + LANES] = acc
            s1 = s1 + acc
        mu = jnp.sum(s1, axis=-1, keepdims=True) * inv_n
        s2 = jnp.zeros((rc, LANES), F32)
        for c0 in lane_chunks:
            dlt = cbuf[rows, c0:c0 + LANES] - mu
            s2 = s2 + dlt * dlt
        var = jnp.sum(s2, axis=-1, keepdims=True) * inv_n
        rstd = lax.rsqrt(var + EPS)
        for c0 in lane_chunks:
            yn = (cbuf[rows, c0:c0 + LANES] - mu) * rstd * lng_ref[:, c0:c0 + LANES] + lnb_ref[:, c0:c0 + LANES]
            y_ref[rows, d_seg + c0:d_seg + c0 + LANES] = (yn * jax.nn.sigmoid(yn)).astype(BF16)

    for j in range(d_seg // tn):
        project(j)
    for r0 in range(0, tm, rc):
        convolve(r0)


def _mixer(hs, hs_meta, norm_g, w_in, b_in, conv_sc_w, conv_cf_w, conv_cf_b, ln_g, ln_b,
           *, seq, tm, tn, name):
    m, d = hs.shape
    d_seg = w_in.shape[1] // 5
    n_meta = hs_meta.shape[0]
    rc = CONV_ROW_CHUNK
    assert m % tm == 0 and seq % tm == 0 and tm % rc == 0 and d_seg % tn == 0
    assert n_meta <= HALO and (HALO - n_meta) % SUBLANES == 0
    n_tiles = m // tm
    full = lambda shape: pl.BlockSpec(shape, lambda i: (0, 0))
    ring = (2, HALO + tm, d_seg)
    return pl.pallas_call(
        functools.partial(_mixer_body, n_tiles=n_tiles, tm=tm, tn=tn, tiles_per_seq=seq // tm, rc=rc),
        grid=(n_tiles + 1,),
        in_specs=[pl.BlockSpec((tm, d), lambda i: (jnp.minimum(i, n_tiles - 1), 0)),
                  full((n_meta, d)),
                  full((1, d)),
                  pl.BlockSpec((d, 5 * d_seg), lambda i: (0, 0), pipeline_mode=pl.Buffered(1)),
                  full((1, 5 * d_seg)),
                  full((SC_WIDTH, d_seg)), full((CF_WIDTH, d_seg)),
                  full((1, d_seg)), full((1, d_seg)), full((1, d_seg))],
        out_specs=pl.BlockSpec((tm, 2 * d_seg), lambda i: (jnp.maximum(i - 1, 0), 0)),
        out_shape=jax.ShapeDtypeStruct((m, 2 * d_seg), BF16),
        scratch_shapes=[pltpu.VMEM((tm, d), BF16),
                        pltpu.VMEM((2, tm, d_seg), F32),
                        pltpu.VMEM(ring, F32),
                        pltpu.VMEM(ring, F32),
                        pltpu.VMEM((HALO, d_seg), F32),
                        pltpu.VMEM((HALO, d_seg), F32),
                        pltpu.VMEM((tm, d_seg), F32)],
        compiler_params=pltpu.CompilerParams(
            dimension_semantics=("arbitrary",),
            vmem_limit_bytes=VMEM_LIMIT),
        name=name,
    )(hs, hs_meta, norm_g.reshape(1, d), w_in, b_in.reshape(1, 5 * d_seg), conv_sc_w, conv_cf_w,
      conv_cf_b.reshape(1, d_seg), ln_g.reshape(1, d_seg), ln_b.reshape(1, d_seg))


def _outproj_body(hs_ref, y_ref, wo_ref, o_ref):
    o_ref[...] = hs_ref[...] + jnp.dot(y_ref[...], wo_ref[...], preferred_element_type=F32)


def _out_proj(hs, y, w_out, *, tm, name):
    m, d = hs.shape
    k = y.shape[1]
    assert m % tm == 0
    return pl.pallas_call(
        _outproj_body,
        grid=(m // tm,),
        in_specs=[pl.BlockSpec((tm, d), lambda i: (i, 0)),
                  pl.BlockSpec((tm, k), lambda i: (i, 0)),
                  pl.BlockSpec((k, d), lambda i: (0, 0), pipeline_mode=pl.Buffered(1))],
        out_specs=pl.BlockSpec((tm, d), lambda i: (i, 0)),
        out_shape=jax.ShapeDtypeStruct((m, d), F32),
        compiler_params=pltpu.CompilerParams(
            dimension_semantics=("arbitrary",),
            vmem_limit_bytes=VMEM_LIMIT),
        name=name,
    )(hs, y, w_out)


def kernel(x, meta_tokens, ffn1_norm, ffn1_w_gate, ffn1_w_up, ffn1_w_down, mix_norm, w_in, b_in,
           conv_sc_w, conv_cf_w, conv_cf_b, ln_cf_g, ln_cf_b, w_out,
           ffn2_norm, ffn2_w_gate, ffn2_w_up, ffn2_w_down, final_norm):
    bsz, seq, d = x.shape
    assert ffn1_norm.shape[0] == 1 and meta_tokens.shape[0] == N_META
    hs = x.reshape(bsz * seq, d)
    hm = meta_tokens.astype(x.dtype)
    w1 = (ffn1_w_gate[0], ffn1_w_up[0], ffn1_w_down[0])
    w2 = (ffn2_w_gate[0], ffn2_w_up[0], ffn2_w_down[0])
    wi = w_in[0].astype(BF16)
    wo = w_out[0].astype(BF16)

    hs, hm = _ffn(hs, ffn1_norm[0], *w1, x_meta=hm, tm=1024, tf=FFN_TF, name="ffn1")

    y = _mixer(hs, hm, mix_norm[0], wi, b_in[0], conv_sc_w[0], conv_cf_w[0], conv_cf_b[0],
               ln_cf_g[0], ln_cf_b[0], seq=seq, tm=512, tn=256, name="mixer")
    hs = _out_proj(hs, y, wo, tm=1024, name="out_proj")
    out = _ffn(hs, ffn2_norm[0], *w2, final_norm, tm=1024, tf=FFN_TF, name="ffn2_final")
    return out.reshape(bsz, seq, d)
```

```python
import functools

import jax
import jax.numpy as jnp
from jax import lax
from jax.experimental import pallas as pl
from jax.experimental.pallas import tpu as pltpu

N_META = 16
SC_WIDTH = 3
CF_WIDTH = 31
FFN_RES_SCALE = 0.5
EPS = 1e-6

SUBLANES = 8
LANES = 128
FFN_TM = 1024
FFN_TF = 512
MIX_TM = 512
MIX_TN = 256
OUT_TM = 1024
CONV_ROW_CHUNK = 64
HALO = 32
V7X_VMEM_BYTES = 64 * 1024 * 1024
VMEM_LIMIT = V7X_VMEM_BYTES - 8 * 1024 * 1024

F32 = jnp.float32
BF16 = jnp.bfloat16


def _rmsnorm_f32(x, g):
    ms = jnp.mean(x * x, axis=-1, keepdims=True)
    return x * lax.rsqrt(ms + EPS) * g


def _ffn_body(x_hbm, g_ref, wg_hbm, wu_hbm, wd_hbm, *rest, n_tiles, n_f, tm, tf, final, meta):
    rest = list(rest)
    fg_ref = rest.pop(0) if final else None
    xm_ref = rest.pop(0) if meta else None
    o_hbm = rest.pop(0)
    om_ref = rest.pop(0) if meta else None
    acc, xn_ref, wg_buf, wu_buf, wd_buf = rest[:5]
    rest = rest[5:]
    xnm_ref = rest.pop(0) if meta else None
    xsem, osem, wsem = rest
    i = pl.program_id(0)
    slot = i % 2

    def x_copy(tile, s):
        return pltpu.make_async_copy(x_hbm.at[pl.ds(tile * tm, tm), :], acc.at[s], xsem.at[s])

    def o_copy(tile, s):
        return pltpu.make_async_copy(acc.at[s], o_hbm.at[pl.ds(tile * tm, tm), :], osem.at[s])

    def w_copies(f, s):
        return (pltpu.make_async_copy(wg_hbm.at[:, pl.ds(f * tf, tf)], wg_buf.at[s], wsem.at[0, s]),
                pltpu.make_async_copy(wu_hbm.at[:, pl.ds(f * tf, tf)], wu_buf.at[s], wsem.at[1, s]),
                pltpu.make_async_copy(wd_hbm.at[pl.ds(f * tf, tf), :], wd_buf.at[s], wsem.at[2, s]))

    @pl.when(i == 0)
    def _():
        x_copy(0, 0).start()
        for c in w_copies(0, 0):
            c.start()
        if meta:
            xnm_ref[...] = _rmsnorm_f32(xm_ref[...], g_ref[...]).astype(BF16)
            om_ref[...] = xm_ref[...]

    x_copy(i, slot).wait()
    xn_ref[...] = _rmsnorm_f32(acc[slot], g_ref[...]).astype(BF16)

    def half_swiglu(xn, wslot):
        gate = jnp.dot(xn, wg_buf[wslot].astype(BF16), preferred_element_type=F32)
        up = jnp.dot(xn, wu_buf[wslot].astype(BF16), preferred_element_type=F32)
        h = (gate * jax.nn.sigmoid(gate) * up).astype(BF16)
        return FFN_RES_SCALE * jnp.dot(h, wd_buf[wslot].astype(BF16), preferred_element_type=F32)

    def f_step(f, carry):
        wslot = (i * n_f + f) % 2
        for c in w_copies(f, wslot):
            c.wait()

        @pl.when(jnp.logical_or(f + 1 < n_f, i + 1 < n_tiles))
        def _():
            for c in w_copies(jnp.where(f + 1 < n_f, f + 1, 0), 1 - wslot):
                c.start()

        @pl.when(jnp.logical_and(f == min(1, n_f - 1), i + 1 < n_tiles))
        def _():
            @pl.when(i >= 1)
            def _():
                o_copy(i - 1, 1 - slot).wait()
            x_copy(i + 1, 1 - slot).start()

        acc[slot] += half_swiglu(xn_ref[...], wslot)

        if meta:
            @pl.when(i == 0)
            def _():
                om_ref[...] += half_swiglu(xnm_ref[...], wslot)
        return carry

    lax.fori_loop(0, n_f, f_step, 0)

    if final:
        acc[slot] = _rmsnorm_f32(acc[slot], fg_ref[...])
    o_copy(i, slot).start()

    @pl.when(i == n_tiles - 1)
    def _():
        if n_tiles > 1:
            o_copy(i - 1, 1 - slot).wait()
        o_copy(i, slot).wait()


def _ffn(x, norm_g, w_gate, w_up, w_down, final_g=None, x_meta=None, *, tm, tf, name):
    m, d = x.shape
    d_ff = w_gate.shape[1]
    assert m % tm == 0 and d_ff % tf == 0
    n_tiles, n_f = m // tm, d_ff // tf
    final, meta = final_g is not None, x_meta is not None
    any_spec = pl.BlockSpec(memory_space=pl.ANY)
    row_spec = pl.BlockSpec((1, d), lambda i: (0, 0))
    in_specs = [any_spec, row_spec, any_spec, any_spec, any_spec]
    args = [x, norm_g.reshape(1, d), w_gate, w_up, w_down]
    out_specs, out_shape = [any_spec], [jax.ShapeDtypeStruct((m, d), F32)]
    scratch = [pltpu.VMEM((2, tm, d), F32),
               pltpu.VMEM((tm, d), BF16),
               pltpu.VMEM((2, d, tf), w_gate.dtype),
               pltpu.VMEM((2, d, tf), w_up.dtype),
               pltpu.VMEM((2, tf, d), w_down.dtype)]
    if final:
        in_specs.append(row_spec)
        args.append(final_g.reshape(1, d))
    if meta:
        meta_spec = pl.BlockSpec(x_meta.shape, lambda i: (0, 0))
        in_specs.append(meta_spec)
        args.append(x_meta)
        out_specs.append(meta_spec)
        out_shape.append(jax.ShapeDtypeStruct(x_meta.shape, F32))
        scratch.append(pltpu.VMEM(x_meta.shape, BF16))
    scratch += [pltpu.SemaphoreType.DMA((2,)), pltpu.SemaphoreType.DMA((2,)), pltpu.SemaphoreType.DMA((3, 2))]
    outs = pl.pallas_call(
        functools.partial(_ffn_body, n_tiles=n_tiles, n_f=n_f, tm=tm, tf=tf, final=final, meta=meta),
        grid=(n_tiles,),
        in_specs=in_specs,
        out_specs=out_specs,
        out_shape=out_shape,
        scratch_shapes=scratch,
        compiler_params=pltpu.CompilerParams(
            dimension_semantics=("arbitrary",),
            vmem_limit_bytes=VMEM_LIMIT),
        name=name,
    )(*args)
    return outs if meta else outs[0]


def _causal_taps(buf, w_ref, r0, c0, rc, width):
    base = HALO - (width - 1)
    out = None
    for s in range(SUBLANES):
        ks = [k for k in range(width) if (base + k) % SUBLANES == s]
        if not ks:
            continue
        rows = rc if s == 0 else rc + SUBLANES
        u = None
        for k in ks:
            a = base + k - s + r0
            t = buf[a:a + rows, c0:c0 + LANES] * w_ref[k:k + 1, c0:c0 + LANES]
            u = t if u is None else u + t
        if s:
            u = u[s:s + rc]
        out = u if out is None else out + u
    return out


def _mixer_body(x_ref, xm_ref, g_ref, win_ref, bias_ref, wsc_ref, wcf_ref, bcf_ref, lng_ref, lnb_ref,
                y_ref, xn_ref, bring, cvring, zring, cvhalo, zhalo, cbuf,
                *, n_tiles, tm, tn, tiles_per_seq, rc):
    i = pl.program_id(0)
    sp = i % 2
    sc = 1 - sp
    d_seg = zring.shape[2]
    n_meta = xm_ref.shape[0]

    @pl.when(i == 0)
    def _():
        bring[1] = jnp.zeros(bring.shape[1:], F32)
        cvring[1] = jnp.zeros(cvring.shape[1:], F32)
        zring[1] = jnp.zeros(zring.shape[1:], F32)

        xm = _rmsnorm_f32(xm_ref[...], g_ref[...]).astype(BF16)

        def proj_meta(s):
            cols = slice(s * d_seg, (s + 1) * d_seg)
            return jnp.dot(xm, win_ref[:, cols], preferred_element_type=F32) + bias_ref[:, cols]
        cvhalo[0:HALO - n_meta, :] = jnp.zeros((HALO - n_meta, d_seg), F32)
        zhalo[0:HALO - n_meta, :] = jnp.zeros((HALO - n_meta, d_seg), F32)
        cvhalo[HALO - n_meta:, :] = proj_meta(1) * proj_meta(2)
        zhalo[HALO - n_meta:, :] = proj_meta(3) * jax.nn.sigmoid(proj_meta(4))

    first = jnp.minimum(i, n_tiles - 1) % tiles_per_seq == 0
    cvring[sp, 0:HALO, :] = jnp.where(first, cvhalo[...], cvring[sc, tm:tm + HALO, :])
    zring[sp, 0:HALO, :] = jnp.where(first, zhalo[...], zring[sc, tm:tm + HALO, :])

    xn_ref[...] = _rmsnorm_f32(x_ref[...], g_ref[...]).astype(BF16)

    def project(j):
        def proj(s):
            cols = slice(s * d_seg + j * tn, s * d_seg + (j + 1) * tn)
            return jnp.dot(xn_ref[...], win_ref[:, cols], preferred_element_type=F32) + bias_ref[:, cols]
        cols = slice(j * tn, (j + 1) * tn)
        bring[sp, :, cols] = proj(0)
        cvring[sp, HALO:, cols] = proj(1) * proj(2)
        zring[sp, HALO:, cols] = proj(3) * jax.nn.sigmoid(proj(4))

    cvbuf, zbuf, bbuf = cvring.at[sc], zring.at[sc], bring.at[sc]
    inv_n = 1.0 / d_seg
    lane_chunks = range(0, d_seg, LANES)

    def convolve(r0):
        rows = slice(r0, r0 + rc)
        for c0 in lane_chunks:
            acc = _causal_taps(cvbuf, wsc_ref, r0, c0, rc, SC_WIDTH)
            y_ref[rows, c0:c0 + LANES] = (bbuf[rows, c0:c0 + LANES] * acc).astype(BF16)

        s1 = jnp.zeros((rc, LANES), F32)
        for c0 in lane_chunks:
            acc = _causal_taps(zbuf, wcf_ref, r0, c0, rc, CF_WIDTH) + bcf_ref[:, c0:c0 + LANES]
            cbuf[rows, c0:c0 + LANES] = acc
            s1 = s1 + acc
        mu = jnp.sum(s1, axis=-1, keepdims=True) * inv_n
        s2 = jnp.zeros((rc, LANES), F32)
        for c0 in lane_chunks:
            dlt = cbuf[rows, c0:c0 + LANES] - mu
            s2 = s2 + dlt * dlt
        var = jnp.sum(s2, axis=-1, keepdims=True) * inv_n
        rstd = lax.rsqrt(var + EPS)
        for c0 in lane_chunks:
            yn = (cbuf[rows, c0:c0 + LANES] - mu) * rstd * lng_ref[:, c0:c0 + LANES] + lnb_ref[:, c0:c0 + LANES]
            y_ref[rows, d_seg + c0:d_seg + c0 + LANES] = (yn * jax.nn.sigmoid(yn)).astype(BF16)

    for j in range(d_seg // tn):
        project(j)
    for r0 in range(0, tm, rc):
        convolve(r0)


def _mixer(hs, hs_meta, norm_g, w_in, b_in, conv_sc_w, conv_cf_w, conv_cf_b, ln_g, ln_b,
           *, seq, tm, tn, name):
    m, d = hs.shape
    d_seg = w_in.shape[1] // 5
    n_meta = hs_meta.shape[0]
    rc = CONV_ROW_CHUNK
    assert m % tm == 0 and seq % tm == 0 and tm % rc == 0 and d_seg % tn == 0
    assert n_meta <= HALO and (HALO - n_meta) % SUBLANES == 0
    n_tiles = m // tm
    ring = (2, HALO + tm, d_seg)
    step = functools.partial(_mixer_body, n_tiles=n_tiles, tm=tm, tn=tn, tiles_per_seq=seq // tm, rc=rc)
    n_resident = 9

    def body(x_hbm, *rest):
        resident, y_hbm, scratch = rest[:n_resident], rest[n_resident], rest[n_resident + 1:]
        pltpu.emit_pipeline(
            lambda x_ref, y_ref: step(x_ref, *resident, y_ref, *scratch),
            grid=(n_tiles + 1,),
            in_specs=[pl.BlockSpec((tm, d), lambda i: (jnp.minimum(i, n_tiles - 1), 0))],
            out_specs=[pl.BlockSpec((tm, 2 * d_seg), lambda i: (jnp.maximum(i - 1, 0), 0))],
        )(x_hbm, y_hbm)

    any_spec = pl.BlockSpec(memory_space=pl.ANY)
    vmem_spec = pl.BlockSpec(memory_space=pltpu.VMEM)
    return pl.pallas_call(
        body,
        in_specs=[any_spec] + [vmem_spec] * n_resident,
        out_specs=any_spec,
        out_shape=jax.ShapeDtypeStruct((m, 2 * d_seg), BF16),
        scratch_shapes=[pltpu.VMEM((tm, d), BF16),
                        pltpu.VMEM((2, tm, d_seg), F32),
                        pltpu.VMEM(ring, F32),
                        pltpu.VMEM(ring, F32),
                        pltpu.VMEM((HALO, d_seg), F32),
                        pltpu.VMEM((HALO, d_seg), F32),
                        pltpu.VMEM((tm, d_seg), F32)],
        compiler_params=pltpu.CompilerParams(vmem_limit_bytes=VMEM_LIMIT),
        name=name,
    )(hs, hs_meta, norm_g.reshape(1, d), w_in, b_in.reshape(1, 5 * d_seg), conv_sc_w, conv_cf_w,
      conv_cf_b.reshape(1, d_seg), ln_g.reshape(1, d_seg), ln_b.reshape(1, d_seg))


def _outproj_body(hs_hbm, y_hbm, wo_ref, o_hbm, *, tm, n_tiles):
    d, k = hs_hbm.shape[1], y_hbm.shape[1]

    def tile(hs_ref, y_ref, o_ref):
        o_ref[...] = hs_ref[...] + jnp.dot(y_ref[...], wo_ref[...], preferred_element_type=F32)

    pltpu.emit_pipeline(
        tile,
        grid=(n_tiles,),
        in_specs=[pl.BlockSpec((tm, d), lambda i: (i, 0)), pl.BlockSpec((tm, k), lambda i: (i, 0))],
        out_specs=[pl.BlockSpec((tm, d), lambda i: (i, 0))],
    )(hs_hbm, y_hbm, o_hbm)


def _out_proj(hs, y, w_out, *, tm, name):
    m, d = hs.shape
    k = y.shape[1]
    assert m % tm == 0
    any_spec = pl.BlockSpec(memory_space=pl.ANY)
    return pl.pallas_call(
        functools.partial(_outproj_body, tm=tm, n_tiles=m // tm),
        in_specs=[any_spec, any_spec, pl.BlockSpec(memory_space=pltpu.VMEM)],
        out_specs=any_spec,
        out_shape=jax.ShapeDtypeStruct((m, d), F32),
        compiler_params=pltpu.CompilerParams(vmem_limit_bytes=VMEM_LIMIT),
        name=name,
    )(hs, y, w_out)


def kernel(x, meta_tokens, ffn1_norm, ffn1_w_gate, ffn1_w_up, ffn1_w_down, mix_norm, w_in, b_in,
           conv_sc_w, conv_cf_w, conv_cf_b, ln_cf_g, ln_cf_b, w_out,
           ffn2_norm, ffn2_w_gate, ffn2_w_up, ffn2_w_down, final_norm):
    bsz, seq, d = x.shape
    assert ffn1_norm.shape[0] == 1 and meta_tokens.shape[0] == N_META
    hs = x.reshape(bsz * seq, d)
    hm = meta_tokens.astype(x.dtype)
    w1 = (ffn1_w_gate[0], ffn1_w_up[0], ffn1_w_down[0])
    w2 = (ffn2_w_gate[0], ffn2_w_up[0], ffn2_w_down[0])
    wi = w_in[0].astype(BF16)
    wo = w_out[0].astype(BF16)

    hs, hm = _ffn(hs, ffn1_norm[0], *w1, x_meta=hm, tm=FFN_TM, tf=FFN_TF, name="ffn1")

    y = _mixer(hs, hm, mix_norm[0], wi, b_in[0], conv_sc_w[0], conv_cf_w[0], conv_cf_b[0],
               ln_cf_g[0], ln_cf_b[0], seq=seq, tm=MIX_TM, tn=MIX_TN, name="mixer")
    hs = _out_proj(hs, y, wo, tm=OUT_TM, name="out_proj")
    out = _ffn(hs, ffn2_norm[0], *w2, final_norm, tm=FFN_TM, tf=FFN_TF, name="ffn2_final")
    return out.reshape(bsz, seq, d)
```

```python
import functools

import jax
import jax.numpy as jnp
from jax import lax
from jax.experimental import pallas as pl
from jax.experimental.pallas import tpu as pltpu

N_META = 16
SC_WIDTH = 3
CF_WIDTH = 31
FFN_RES_SCALE = 0.5
EPS = 1e-6

SUBLANES = 8
LANES = 128
FFN_TM = 1024
FFN_TF = 512
MIX_TM = 512
MIX_TN = 256
OUT_TM = 1024
CONV_ROW_CHUNK = 64
NORM_ROW_CHUNK = 16
HALO = 32
V7X_VMEM_BYTES = 64 * 1024 * 1024
VMEM_LIMIT = V7X_VMEM_BYTES - 8 * 1024 * 1024

F32 = jnp.float32
BF16 = jnp.bfloat16


def _rmsnorm_f32(x, g):
    ms = jnp.mean(x * x, axis=-1, keepdims=True)
    return x * lax.rsqrt(ms + EPS) * g


def _ffn_body(x_hbm, g_ref, wg_hbm, wu_hbm, wd_hbm, *rest, n_tiles, n_f, tm, tf, final, meta):
    rest = list(rest)
    fg_ref = rest.pop(0) if final else None
    xm_ref = rest.pop(0) if meta else None
    o_hbm = rest.pop(0)
    om_ref = rest.pop(0) if meta else None
    acc, xn_ref, wg_buf, wu_buf, wd_buf = rest[:5]
    rest = rest[5:]
    xnm_ref = rest.pop(0) if meta else None
    xsem, osem, wsem = rest
    i = pl.program_id(0)
    slot = i % 2

    def x_copy(tile, s):
        return pltpu.make_async_copy(x_hbm.at[pl.ds(tile * tm, tm), :], acc.at[s], xsem.at[s])

    def o_copy(tile, s):
        return pltpu.make_async_copy(acc.at[s], o_hbm.at[pl.ds(tile * tm, tm), :], osem.at[s])

    def w_copies(f, s):
        return (pltpu.make_async_copy(wg_hbm.at[:, pl.ds(f * tf, tf)], wg_buf.at[s], wsem.at[0, s]),
                pltpu.make_async_copy(wu_hbm.at[:, pl.ds(f * tf, tf)], wu_buf.at[s], wsem.at[1, s]),
                pltpu.make_async_copy(wd_hbm.at[pl.ds(f * tf, tf), :], wd_buf.at[s], wsem.at[2, s]))

    @pl.when(i == 0)
    def _():
        x_copy(0, 0).start()
        for c in w_copies(0, 0):
            c.start()
        if meta:
            xnm_ref[...] = _rmsnorm_f32(xm_ref[...], g_ref[...]).astype(BF16)
            om_ref[...] = xm_ref[...]

    x_copy(i, slot).wait()
    for r0 in range(0, tm, NORM_ROW_CHUNK):
        rows = slice(r0, r0 + NORM_ROW_CHUNK)
        xn_ref[rows, :] = _rmsnorm_f32(acc[slot, rows, :], g_ref[...]).astype(BF16)

    def half_swiglu(xn, wslot):
        gate = jnp.dot(xn, wg_buf[wslot].astype(BF16), preferred_element_type=F32)
        up = jnp.dot(xn, wu_buf[wslot].astype(BF16), preferred_element_type=F32)
        h = (gate * jax.nn.sigmoid(gate) * up).astype(BF16)
        return FFN_RES_SCALE * jnp.dot(h, wd_buf[wslot].astype(BF16), preferred_element_type=F32)

    def f_step(f, carry):
        wslot = (i * n_f + f) % 2
        for c in w_copies(f, wslot):
            c.wait()

        @pl.when(jnp.logical_or(f + 1 < n_f, i + 1 < n_tiles))
        def _():
            for c in w_copies(jnp.where(f + 1 < n_f, f + 1, 0), 1 - wslot):
                c.start()

        @pl.when(jnp.logical_and(f == min(1, n_f - 1), i + 1 < n_tiles))
        def _():
            @pl.when(i >= 1)
            def _():
                o_copy(i - 1, 1 - slot).wait()
            x_copy(i + 1, 1 - slot).start()

        acc[slot] += half_swiglu(xn_ref[...], wslot)

        if meta:
            @pl.when(i == 0)
            def _():
                om_ref[...] += half_swiglu(xnm_ref[...], wslot)
        return carry

    lax.fori_loop(0, n_f, f_step, 0)

    if final:
        for r0 in range(0, tm, NORM_ROW_CHUNK):
            rows = slice(r0, r0 + NORM_ROW_CHUNK)
            acc[slot, rows, :] = _rmsnorm_f32(acc[slot, rows, :], fg_ref[...])
    o_copy(i, slot).start()

    @pl.when(i == n_tiles - 1)
    def _():
        if n_tiles > 1:
            o_copy(i - 1, 1 - slot).wait()
        o_copy(i, slot).wait()


def _ffn(x, norm_g, w_gate, w_up, w_down, final_g=None, x_meta=None, *, tm, tf, name):
    m, d = x.shape
    d_ff = w_gate.shape[1]
    assert m % tm == 0 and d_ff % tf == 0
    n_tiles, n_f = m // tm, d_ff // tf
    final, meta = final_g is not None, x_meta is not None
    any_spec = pl.BlockSpec(memory_space=pl.ANY)
    row_spec = pl.BlockSpec((1, d), lambda i: (0, 0))
    in_specs = [any_spec, row_spec, any_spec, any_spec, any_spec]
    args = [x, norm_g.reshape(1, d), w_gate, w_up, w_down]
    out_specs, out_shape = [any_spec], [jax.ShapeDtypeStruct((m, d), F32)]
    scratch = [pltpu.VMEM((2, tm, d), F32),
               pltpu.VMEM((tm, d), BF16),
               pltpu.VMEM((2, d, tf), w_gate.dtype),
               pltpu.VMEM((2, d, tf), w_up.dtype),
               pltpu.VMEM((2, tf, d), w_down.dtype)]
    if final:
        in_specs.append(row_spec)
        args.append(final_g.reshape(1, d))
    if meta:
        meta_spec = pl.BlockSpec(x_meta.shape, lambda i: (0, 0))
        in_specs.append(meta_spec)
        args.append(x_meta)
        out_specs.append(meta_spec)
        out_shape.append(jax.ShapeDtypeStruct(x_meta.shape, F32))
        scratch.append(pltpu.VMEM(x_meta.shape, BF16))
    scratch += [pltpu.SemaphoreType.DMA((2,)), pltpu.SemaphoreType.DMA((2,)), pltpu.SemaphoreType.DMA((3, 2))]
    outs = pl.pallas_call(
        functools.partial(_ffn_body, n_tiles=n_tiles, n_f=n_f, tm=tm, tf=tf, final=final, meta=meta),
        grid=(n_tiles,),
        in_specs=in_specs,
        out_specs=out_specs,
        out_shape=out_shape,
        scratch_shapes=scratch,
        compiler_params=pltpu.CompilerParams(
            dimension_semantics=("arbitrary",),
            vmem_limit_bytes=VMEM_LIMIT),
        name=name,
    )(*args)
    return outs if meta else outs[0]


def _causal_taps(buf, w_ref, r0, c0, rc, width):
    base = HALO - (width - 1)
    out = None
    for s in range(SUBLANES):
        ks = [k for k in range(width) if (base + k) % SUBLANES == s]
        if not ks:
            continue
        rows = rc if s == 0 else rc + SUBLANES
        u = None
        for k in ks:
            a = base + k - s + r0
            t = buf[a:a + rows, c0:c0 + LANES] * w_ref[k:k + 1, c0:c0 + LANES]
            u = t if u is None else u + t
        if s:
            u = u[s:s + rc]
        out = u if out is None else out + u
    return out


def _mixer_body(x_ref, xm_ref, g_ref, win_ref, bias_ref, wsc_ref, wcf_ref, bcf_ref, lng_ref, lnb_ref,
                y_ref, xn_ref, bring, cvring, zring, cvhalo, zhalo, cbuf,
                *, n_tiles, tm, tn, tiles_per_seq, rc):
    i = pl.program_id(0)
    sp = i % 2
    sc = 1 - sp
    d_seg = zring.shape[2]
    n_meta = xm_ref.shape[0]

    @pl.when(i == 0)
    def _():
        bring[1] = jnp.zeros(bring.shape[1:], F32)
        cvring[1] = jnp.zeros(cvring.shape[1:], F32)
        zring[1] = jnp.zeros(zring.shape[1:], F32)

        xm = _rmsnorm_f32(xm_ref[...], g_ref[...]).astype(BF16)

        def proj_meta(s):
            cols = slice(s * d_seg, (s + 1) * d_seg)
            return jnp.dot(xm, win_ref[:, cols], preferred_element_type=F32) + bias_ref[:, cols]
        cvhalo[0:HALO - n_meta, :] = jnp.zeros((HALO - n_meta, d_seg), F32)
        zhalo[0:HALO - n_meta, :] = jnp.zeros((HALO - n_meta, d_seg), F32)
        cvhalo[HALO - n_meta:, :] = proj_meta(1) * proj_meta(2)
        zhalo[HALO - n_meta:, :] = proj_meta(3) * jax.nn.sigmoid(proj_meta(4))

    first = jnp.minimum(i, n_tiles - 1) % tiles_per_seq == 0
    cvring[sp, 0:HALO, :] = jnp.where(first, cvhalo[...], cvring[sc, tm:tm + HALO, :])
    zring[sp, 0:HALO, :] = jnp.where(first, zhalo[...], zring[sc, tm:tm + HALO, :])

    for r0 in range(0, tm, NORM_ROW_CHUNK):
        rows = slice(r0, r0 + NORM_ROW_CHUNK)
        xn_ref[rows, :] = _rmsnorm_f32(x_ref[rows, :], g_ref[...]).astype(BF16)

    def project(j):
        def proj(s):
            cols = slice(s * d_seg + j * tn, s * d_seg + (j + 1) * tn)
            return jnp.dot(xn_ref[...], win_ref[:, cols], preferred_element_type=F32) + bias_ref[:, cols]
        cols = slice(j * tn, (j + 1) * tn)
        bring[sp, :, cols] = proj(0)
        cvring[sp, HALO:, cols] = proj(1) * proj(2)
        zring[sp, HALO:, cols] = proj(3) * jax.nn.sigmoid(proj(4))

    cvbuf, zbuf, bbuf = cvring.at[sc], zring.at[sc], bring.at[sc]
    inv_n = 1.0 / d_seg
    lane_chunks = range(0, d_seg, LANES)

    def convolve(r0):
        rows = slice(r0, r0 + rc)
        for c0 in lane_chunks:
            acc = _causal_taps(cvbuf, wsc_ref, r0, c0, rc, SC_WIDTH)
            y_ref[rows, c0:c0 + LANES] = (bbuf[rows, c0:c0 + LANES] * acc).astype(BF16)

        s1 = jnp.zeros((rc, LANES), F32)
        for c0 in lane_chunks:
            acc = _causal_taps(zbuf, wcf_ref, r0, c0, rc, CF_WIDTH) + bcf_ref[:, c0:c0 + LANES]
            cbuf[rows, c0:c0 + LANES] = acc
            s1 = s1 + acc
        mu = jnp.sum(s1, axis=-1, keepdims=True) * inv_n
        s2 = jnp.zeros((rc, LANES), F32)
        for c0 in lane_chunks:
            dlt = cbuf[rows, c0:c0 + LANES] - mu
            s2 = s2 + dlt * dlt
        var = jnp.sum(s2, axis=-1, keepdims=True) * inv_n
        rstd = lax.rsqrt(var + EPS)
        for c0 in lane_chunks:
            yn = (cbuf[rows, c0:c0 + LANES] - mu) * rstd * lng_ref[:, c0:c0 + LANES] + lnb_ref[:, c0:c0 + LANES]
            y_ref[rows, d_seg + c0:d_seg + c0 + LANES] = (yn * jax.nn.sigmoid(yn)).astype(BF16)

    for j in range(d_seg // tn):
        project(j)
    for r0 in range(0, tm, rc):
        convolve(r0)


def _mixer(hs, hs_meta, norm_g, w_in, b_in, conv_sc_w, conv_cf_w, conv_cf_b, ln_g, ln_b,
           *, seq, tm, tn, name):
    m, d = hs.shape
    d_seg = w_in.shape[1] // 5
    n_meta = hs_meta.shape[0]
    rc = CONV_ROW_CHUNK
    assert m % tm == 0 and seq % tm == 0 and tm % rc == 0 and d_seg % tn == 0
    assert n_meta <= HALO and (HALO - n_meta) % SUBLANES == 0
    n_tiles = m // tm
    full = lambda shape: pl.BlockSpec(shape, lambda i: (0, 0))
    ring = (2, HALO + tm, d_seg)
    return pl.pallas_call(
        functools.partial(_mixer_body, n_tiles=n_tiles, tm=tm, tn=tn, tiles_per_seq=seq // tm, rc=rc),
        grid=(n_tiles + 1,),
        in_specs=[pl.BlockSpec((tm, d), lambda i: (jnp.minimum(i, n_tiles - 1), 0)),
                  full((n_meta, d)),
                  full((1, d)),
                  pl.BlockSpec((d, 5 * d_seg), lambda i: (0, 0), pipeline_mode=pl.Buffered(1)),
                  full((1, 5 * d_seg)),
                  full((SC_WIDTH, d_seg)), full((CF_WIDTH, d_seg)),
                  full((1, d_seg)), full((1, d_seg)), full((1, d_seg))],
        out_specs=pl.BlockSpec((tm, 2 * d_seg), lambda i: (jnp.maximum(i - 1, 0), 0)),
        out_shape=jax.ShapeDtypeStruct((m, 2 * d_seg), BF16),
        scratch_shapes=[pltpu.VMEM((tm, d), BF16),
                        pltpu.VMEM((2, tm, d_seg), F32),
                        pltpu.VMEM(ring, F32),
                        pltpu.VMEM(ring, F32),
                        pltpu.VMEM((HALO, d_seg), F32),
                        pltpu.VMEM((HALO, d_seg), F32),
                        pltpu.VMEM((tm, d_seg), F32)],
        compiler_params=pltpu.CompilerParams(
            dimension_semantics=("arbitrary",),
            vmem_limit_bytes=VMEM_LIMIT),
        name=name,
    )(hs, hs_meta, norm_g.reshape(1, d), w_in, b_in.reshape(1, 5 * d_seg), conv_sc_w, conv_cf_w,
      conv_cf_b.reshape(1, d_seg), ln_g.reshape(1, d_seg), ln_b.reshape(1, d_seg))


def _outproj_body(hs_ref, y_ref, wo_ref, o_ref):
    o_ref[...] = hs_ref[...] + jnp.dot(y_ref[...], wo_ref[...], preferred_element_type=F32)


def _out_proj(hs, y, w_out, *, tm, name):
    m, d = hs.shape
    k = y.shape[1]
    assert m % tm == 0
    return pl.pallas_call(
        _outproj_body,
        grid=(m // tm,),
        in_specs=[pl.BlockSpec((tm, d), lambda i: (i, 0)),
                  pl.BlockSpec((tm, k), lambda i: (i, 0)),
                  pl.BlockSpec((k, d), lambda i: (0, 0), pipeline_mode=pl.Buffered(1))],
        out_specs=pl.BlockSpec((tm, d), lambda i: (i, 0)),
        out_shape=jax.ShapeDtypeStruct((m, d), F32),
        compiler_params=pltpu.CompilerParams(
            dimension_semantics=("arbitrary",),
            vmem_limit_bytes=VMEM_LIMIT),
        name=name,
    )(hs, y, w_out)


def kernel(x, meta_tokens, ffn1_norm, ffn1_w_gate, ffn1_w_up, ffn1_w_down, mix_norm, w_in, b_in,
           conv_sc_w, conv_cf_w, conv_cf_b, ln_cf_g, ln_cf_b, w_out,
           ffn2_norm, ffn2_w_gate, ffn2_w_up, ffn2_w_down, final_norm):
    bsz, seq, d = x.shape
    assert ffn1_norm.shape[0] == 1 and meta_tokens.shape[0] == N_META
    hs = x.reshape(bsz * seq, d)
    hm = meta_tokens.astype(x.dtype)
    w1 = (ffn1_w_gate[0], ffn1_w_up[0], ffn1_w_down[0])
    w2 = (ffn2_w_gate[0], ffn2_w_up[0], ffn2_w_down[0])
    wi = w_in[0].astype(BF16)
    wo = w_out[0].astype(BF16)

    hs, hm = _ffn(hs, ffn1_norm[0], *w1, x_meta=hm, tm=FFN_TM, tf=FFN_TF, name="ffn1")

    y = _mixer(hs, hm, mix_norm[0], wi, b_in[0], conv_sc_w[0], conv_cf_w[0], conv_cf_b[0],
               ln_cf_g[0], ln_cf_b[0], seq=seq, tm=MIX_TM, tn=MIX_TN, name="mixer")
    hs = _out_proj(hs, y, wo, tm=OUT_TM, name="out_proj")
    out = _ffn(hs, ffn2_norm[0], *w2, final_norm, tm=FFN_TM, tf=FFN_TF, name="ffn2_final")
    return out.reshape(bsz, seq, d)
```
